```python
import math
import jax, jax.numpy as jnp
from jax import lax
import numpy as np

D_MODEL = 2048
BATCH = 16
SEQ = 2048
DEPTH = 4

HEAD_DIM = 128
BLOCK = 128
SWA_Q_HEADS = 8
SWA_KV_HEADS = 2
WINDOW = 128
SB_HEADS = 4
MLA_HEADS = 4
MLA_Q_RANK = 512
MLA_KV_RANK = 512
MLA_NOPE = 128
MLA_ROPE = 64
MLA_V = 128
ROPE_THETA = 10000.0
MEM_LEN = 256
XA_HEADS = 4
REL_BUCKETS = 32
REL_MAX_DIST = 128
D_FF = ((8 * D_MODEL + 3 * 256 - 1) // (3 * 256)) * 256
EPS = 1e-6

WA = SWA_Q_HEADS * HEAD_DIM
KVA = SWA_KV_HEADS * HEAD_DIM
WB = SB_HEADS * HEAD_DIM
WC = MLA_HEADS * MLA_V
MIX_W = WA + WB + WC
XA_W = XA_HEADS * HEAD_DIM
MLA_QK = MLA_NOPE + MLA_ROPE
IN_SIZES = (WA, KVA, KVA, WB, WB, WB, MLA_Q_RANK, MLA_KV_RANK, MLA_ROPE, 3 * D_MODEL)
IN_W = WA + 2 * KVA + 3 * WB + MLA_Q_RANK + MLA_KV_RANK + MLA_ROPE + 3 * D_MODEL

kernel_name = 'hybrid_gated_swa_stickbreak_mla_block'


def split_offsets(sizes):
    out, acc = [], 0
    for s in sizes[:-1]:
        acc += s
        out.append(acc)
    return out


def rms_norm(x, g):
    xf = x.astype(jnp.float32)
    y = xf * lax.rsqrt(jnp.mean(xf * xf, axis=-1, keepdims=True) + EPS)
    return (y * g.astype(jnp.float32)).astype(x.dtype)


def rope(x, positions):
    half = x.shape[-1] // 2
    inv = ROPE_THETA ** (-jnp.arange(half, dtype=jnp.float32) / half)
    ang = positions.astype(jnp.float32)[:, :, None, None] * inv
    cos, sin = jnp.cos(ang), jnp.sin(ang)
    xf = x.astype(jnp.float32)
    x1, x2 = xf[..., :half], xf[..., half:]
    return jnp.concatenate([x1 * cos - x2 * sin, x1 * sin + x2 * cos], axis=-1).astype(x.dtype)


def t5_bucket(rel):
    n = jnp.maximum(rel, 0)
    exact = REL_BUCKETS // 2
    nf = jnp.maximum(n, exact).astype(jnp.float32)
    large = exact + (jnp.log(nf / exact) / math.log(REL_MAX_DIST / exact)
                     * (REL_BUCKETS - exact)).astype(jnp.int32)
    large = jnp.minimum(large, REL_BUCKETS - 1)
    return jnp.where(n < exact, n, large)


def sliding_window_attention(q, k, v, bias, sinks):
    b, s, hq, d = q.shape
    hkv = k.shape[2]
    g = hq // hkv
    nb = s // BLOCK
    qb = q.reshape(b, nb, BLOCK, hkv, g, d)

    def band(t):
        tb = t.reshape(b, nb, BLOCK, hkv, d)
        prev = jnp.concatenate([jnp.zeros_like(tb[:, :1]), tb[:, :-1]], axis=1)
        return jnp.concatenate([prev, tb], axis=2)

    kb, vb = band(k), band(v)
    scores = jnp.einsum('bnqhgd,bnkhd->bnhgqk', qb, kb).astype(jnp.float32) * (d ** -0.5)
    scores = scores + bias.reshape(hkv, g, BLOCK, 2 * BLOCK).astype(jnp.float32)
    i = jnp.arange(BLOCK)[:, None]
    j = jnp.arange(2 * BLOCK)[None, :]
    rel = BLOCK + i - j
    in_window = (rel >= 0) & (rel < WINDOW)
    real_key = (jnp.arange(nb)[:, None, None] > 0) | (j >= BLOCK)[None]
    mask = in_window[None] & real_key
    scores = jnp.where(mask[None, :, None, None], scores, -jnp.inf)
    sink = sinks.astype(jnp.float32).reshape(1, 1, hkv, g, 1, 1)
    m = jnp.maximum(scores.max(axis=-1, keepdims=True), sink)
    p = jnp.exp(scores - m)
    probs = p / (p.sum(axis=-1, keepdims=True) + jnp.exp(sink - m))
    out = jnp.einsum('bnhgqk,bnkhd->bnqhgd', probs.astype(v.dtype), vb)
    return out.reshape(b, s, hq * d)


def stick_breaking_attention(q, k, v):
    b, s, h, d = q.shape
    nb = s // BLOCK
    qb = jnp.moveaxis(q.reshape(b, nb, BLOCK, h, d), 1, 0)
    key_pos = jnp.arange(s)

    def one_block(args):
        qblk, blk = args
        z = jnp.einsum('bqhd,bkhd->bhqk', qblk, k).astype(jnp.float32) * (d ** -0.5)
        qpos = blk * BLOCK + jnp.arange(BLOCK)
        earlier = key_pos[None, :] < qpos[:, None]
        log_keep = jnp.where(earlier, jax.nn.log_sigmoid(-z), 0.0)
        between = lax.cumsum(log_keep, axis=3, reverse=True) - log_keep
        weight = jnp.where(earlier, jnp.exp(jax.nn.log_sigmoid(z) + between), 0.0)
        return jnp.einsum('bhqk,bkhd->bqhd', weight.astype(v.dtype), v)

    out = lax.map(one_block, (qb, jnp.arange(nb)))
    return jnp.moveaxis(out, 0, 1).reshape(b, s, h * d)


def mla_attention(q_nope, q_rope, k_nope, k_rope, v):
    b, s, h, _ = q_nope.shape
    nb = s // BLOCK
    scale = MLA_QK ** -0.5
    qn = jnp.moveaxis(q_nope.reshape(b, nb, BLOCK, h, MLA_NOPE), 1, 0)
    qr = jnp.moveaxis(q_rope.reshape(b, nb, BLOCK, h, MLA_ROPE), 1, 0)
    key_pos = jnp.arange(s)

    def one_block(args):
        qn_b, qr_b, blk = args
        scores = (jnp.einsum('bqhn,bkhn->bhqk', qn_b, k_nope).astype(jnp.float32)
                  + jnp.einsum('bqhr,bkr->bhqk', qr_b, k_rope).astype(jnp.float32)) * scale
        qpos = blk * BLOCK + jnp.arange(BLOCK)
        causal = key_pos[None, :] <= qpos[:, None]
        p = jax.nn.softmax(jnp.where(causal, scores, -jnp.inf), axis=-1)
        return jnp.einsum('bhqk,bkhv->bqhv', p.astype(v.dtype), v)

    out = lax.map(one_block, (qn, qr, jnp.arange(nb)))
    return jnp.moveaxis(out, 0, 1).reshape(b, s, h * MLA_V)


def memory_cross_attention(xn, memn, wq, wkv, q_g, k_g, wo):
    b, s, _ = xn.shape
    m = memn.shape[1]
    q = rms_norm((xn @ wq).reshape(b, s, XA_HEADS, HEAD_DIM), q_g)
    k, v = jnp.split((memn @ wkv).reshape(b, m, XA_HEADS, 2 * HEAD_DIM), 2, axis=-1)
    k = rms_norm(k, k_g)
    scores = jnp.einsum('bqhd,bkhd->bhqk', q, k).astype(jnp.float32) * (HEAD_DIM ** -0.5)
    p = jax.nn.softmax(scores, axis=-1)
    o = jnp.einsum('bhqk,bkhd->bqhd', p.astype(v.dtype), v).reshape(b, s, XA_W)
    return o @ wo


def swiglu(hn, w_gu, w_down):
    gate, up = jnp.split(hn @ w_gu, 2, axis=-1)
    return (jax.nn.silu(gate) * up) @ w_down


def setup_inputs(seed: int = 0) -> dict:
    key = jax.random.key(seed)
    ks = jax.random.split(key, 32)

    def nrm(k, shape, fan_in):
        return jax.random.normal(k, shape, jnp.float32) * (fan_in ** -0.5)

    def gain(k, shape):
        return 1.0 + 0.05 * jax.random.normal(k, shape, jnp.float32)

    start = jax.random.randint(ks[2], (BATCH, 1), 0, 4096, dtype=jnp.int32)
    positions = (start + jnp.arange(SEQ, dtype=jnp.int32)[None, :]).astype(jnp.int32)
    return {
        'x': jax.random.normal(ks[0], (BATCH, SEQ, D_MODEL), jnp.float32),
        'mem': jax.random.normal(ks[1], (BATCH, MEM_LEN, D_MODEL), jnp.float32),
        'positions': positions,
        'rel_bias': 0.5 * jax.random.normal(ks[3], (REL_BUCKETS, SWA_Q_HEADS), jnp.float32),
        'norm_mix': gain(ks[4], (DEPTH, D_MODEL)),
        'w_in': nrm(ks[5], (DEPTH, D_MODEL, IN_W), D_MODEL),
        'swa_q_norm': gain(ks[6], (DEPTH, HEAD_DIM)),
        'swa_k_norm': gain(ks[7], (DEPTH, HEAD_DIM)),
        'swa_sinks': jax.random.normal(ks[8], (DEPTH, SWA_Q_HEADS), jnp.float32),
        'mla_cq_norm': gain(ks[9], (DEPTH, MLA_Q_RANK)),
        'mla_ckv_norm': gain(ks[10], (DEPTH, MLA_KV_RANK)),
        'mla_w_uq': nrm(ks[11], (DEPTH, MLA_Q_RANK, MLA_HEADS * MLA_QK), MLA_Q_RANK),
        'mla_w_ukv': nrm(ks[12], (DEPTH, MLA_KV_RANK, MLA_HEADS * (MLA_NOPE + MLA_V)), MLA_KV_RANK),
        'mla_q_norm': gain(ks[13], (DEPTH, MLA_QK)),
        'mla_k_norm': gain(ks[14], (DEPTH, MLA_QK)),
        'w_branch': nrm(ks[15], (DEPTH, MIX_W, D_MODEL), MIX_W),
        'w_out': nrm(ks[16], (DEPTH, D_MODEL, D_MODEL), D_MODEL),
        'norm_xa': gain(ks[17], (DEPTH, D_MODEL)),
        'norm_mem': gain(ks[18], (DEPTH, D_MODEL)),
        'xa_wq': nrm(ks[19], (DEPTH, D_MODEL, XA_W), D_MODEL),
        'xa_wkv': nrm(ks[20], (DEPTH, D_MODEL, 2 * XA_W), D_MODEL),
        'xa_q_norm': gain(ks[21], (DEPTH, HEAD_DIM)),
        'xa_k_norm': gain(ks[22], (DEPTH, HEAD_DIM)),
        'xa_wo': nrm(ks[23], (DEPTH, XA_W, D_MODEL), XA_W),
        'norm_ffn': gain(ks[24], (DEPTH, D_MODEL)),
        'ffn_w_gu': nrm(ks[25], (DEPTH, D_MODEL, 2 * D_FF), D_MODEL),
        'ffn_w_down': nrm(ks[26], (DEPTH, D_FF, D_MODEL), D_FF),
    }


def reference(x, mem, positions, rel_bias, norm_mix, w_in, swa_q_norm, swa_k_norm, swa_sinks,
              mla_cq_norm, mla_ckv_norm, mla_w_uq, mla_w_ukv, mla_q_norm, mla_k_norm,
              w_branch, w_out, norm_xa, norm_mem, xa_wq, xa_wkv, xa_q_norm, xa_k_norm, xa_wo,
              norm_ffn, ffn_w_gu, ffn_w_down):
    b, s, _ = x.shape
    offsets = split_offsets(IN_SIZES)

    i = jnp.arange(BLOCK)[:, None]
    j = jnp.arange(2 * BLOCK)[None, :]
    swa_bias = jnp.transpose(rel_bias[t5_bucket(BLOCK + i - j)], (2, 0, 1))

    for l in range(DEPTH):
        h = rms_norm(x, norm_mix[l])
        qa, ka, va, qb, kb, vb, cq, ckv, kr, gates = jnp.split(h @ w_in[l], offsets, axis=-1)

        qa = rms_norm(qa.reshape(b, s, SWA_Q_HEADS, HEAD_DIM), swa_q_norm[l])
        ka = rms_norm(ka.reshape(b, s, SWA_KV_HEADS, HEAD_DIM), swa_k_norm[l])
        va = va.reshape(b, s, SWA_KV_HEADS, HEAD_DIM)
        o_a = sliding_window_attention(qa, ka, va, swa_bias, swa_sinks[l])

        o_b = stick_breaking_attention(qb.reshape(b, s, SB_HEADS, HEAD_DIM),
                                       kb.reshape(b, s, SB_HEADS, HEAD_DIM),
                                       vb.reshape(b, s, SB_HEADS, HEAD_DIM))

        q_c = (rms_norm(cq, mla_cq_norm[l]) @ mla_w_uq[l]).reshape(b, s, MLA_HEADS, MLA_QK)
        kv_c = (rms_norm(ckv, mla_ckv_norm[l]) @ mla_w_ukv[l]).reshape(b, s, MLA_HEADS, MLA_NOPE + MLA_V)
        qg, kg = mla_q_norm[l], mla_k_norm[l]
        q_nope = rms_norm(q_c[..., :MLA_NOPE], qg[:MLA_NOPE])
        q_rope = rope(rms_norm(q_c[..., MLA_NOPE:], qg[MLA_NOPE:]), positions)
        k_nope = rms_norm(kv_c[..., :MLA_NOPE], kg[:MLA_NOPE])
        v_c = kv_c[..., MLA_NOPE:]
        k_rope = rope(rms_norm(kr, kg[MLA_NOPE:])[:, :, None, :], positions)[:, :, 0, :]
        o_c = mla_attention(q_nope, q_rope, k_nope, k_rope, v_c)

        g_a, g_b, g_c = jnp.split(jax.nn.sigmoid(gates), 3, axis=-1)
        wbr = w_branch[l]
        merged = (g_a * (o_a @ wbr[:WA])
                  + g_b * (o_b @ wbr[WA:WA + WB])
                  + g_c * (o_c @ wbr[WA + WB:]))
        x = x + merged @ w_out[l]

        x = x + memory_cross_attention(rms_norm(x, norm_xa[l]), rms_norm(mem, norm_mem[l]),
                                       xa_wq[l], xa_wkv[l], xa_q_norm[l], xa_k_norm[l], xa_wo[l])

        x = x + swiglu(rms_norm(x, norm_ffn[l]), ffn_w_gu[l], ffn_w_down[l])
    return x
```

```python
import functools
import math

import jax
import jax.numpy as jnp
from jax import lax
from jax.experimental import pallas as pl
from jax.experimental.pallas import tpu as pltpu

F32 = jnp.float32
BF16 = jnp.bfloat16

D_MODEL = 2048
HEAD_DIM = 128
BLOCK = 128
SWA_Q_HEADS = 8
SWA_KV_HEADS = 2
WINDOW = 128
SB_HEADS = 4
MLA_HEADS = 4
MLA_Q_RANK = 512
MLA_KV_RANK = 512
MLA_NOPE = 128
MLA_ROPE = 64
MLA_V = 128
ROPE_THETA = 10000.0
XA_HEADS = 4
REL_BUCKETS = 32
REL_MAX_DIST = 128
D_FF = ((8 * D_MODEL + 3 * 256 - 1) // (3 * 256)) * 256
EPS = 1e-6

WA = SWA_Q_HEADS * HEAD_DIM
KVA = SWA_KV_HEADS * HEAD_DIM
WB = SB_HEADS * HEAD_DIM
WC = MLA_HEADS * MLA_V
XA_W = XA_HEADS * HEAD_DIM
MLA_QK = MLA_NOPE + MLA_ROPE

COL_QA = 0
COL_KA = COL_QA + WA
COL_VA = COL_KA + KVA
COL_QB = COL_VA + KVA
COL_KB = COL_QB + WB
COL_VB = COL_KB + WB
COL_CQ = COL_VB + WB
COL_CKV = COL_CQ + MLA_Q_RANK
COL_GATES = COL_CKV + MLA_KV_RANK
COL_KR = COL_GATES + 3 * D_MODEL
IN_TILE_N = 768
IN_W_PAD = -(-(COL_KR + HEAD_DIM) // IN_TILE_N) * IN_TILE_N

MASK_VALUE = -1e30
V7X_VMEM_LIMIT_BYTES = 56 * 1024 * 1024


def _params(semantics):
    return pltpu.CompilerParams(dimension_semantics=semantics,
                                vmem_limit_bytes=V7X_VMEM_LIMIT_BYTES)


def _rms(x, gain, width):
    ms = jnp.sum(x * x, axis=-1, keepdims=True) * (1.0 / width)
    return x * lax.rsqrt(ms + EPS) * gain


def _dot(a, b):
    return jnp.dot(a, b, preferred_element_type=F32)


def _dot_t(a, b):
    return lax.dot_general(a, b, (((1,), (1,)), ((), ())), preferred_element_type=F32)


def _resident(shape):
    nd = len(shape)
    return pl.BlockSpec(shape, lambda *_: (0,) * nd, pipeline_mode=pl.Buffered(1))


def _rope_table_kernel(pos_ref, inv_ref, cos_ref, sin_ref):
    ang = pos_ref[...].astype(F32) * inv_ref[...]
    lane = lax.broadcasted_iota(jnp.int32, ang.shape, 1)
    first_half = (lane & (MLA_ROPE - 1)) < (MLA_ROPE // 2)
    cos_ref[...] = jnp.cos(ang)
    s = jnp.sin(ang)
    sin_ref[...] = jnp.where(first_half, -s, s)


def _rope_tables(positions):
    t = positions.size
    half = MLA_ROPE // 2
    inv = ROPE_THETA ** (-jnp.arange(half, dtype=F32) / half)
    inv_l = jnp.tile(inv, HEAD_DIM // half).reshape(1, HEAD_DIM)
    pos_l = jnp.broadcast_to(positions.reshape(t, 1), (t, HEAD_DIM))
    tm = 1024
    return pl.pallas_call(
        _rope_table_kernel,
        grid=(t // tm,),
        in_specs=[pl.BlockSpec((tm, HEAD_DIM), lambda i: (i, 0)),
                  pl.BlockSpec((1, HEAD_DIM), lambda i: (0, 0))],
        out_specs=[pl.BlockSpec((tm, HEAD_DIM), lambda i: (i, 0)),
                   pl.BlockSpec((tm, HEAD_DIM), lambda i: (i, 0))],
        out_shape=[jax.ShapeDtypeStruct((t, HEAD_DIM), F32)] * 2,
        compiler_params=_params(("parallel",)),
        name="rope_tables",
    )(pos_l, inv_l)


def _t5_bucket(rel):
    n = jnp.maximum(rel, 0)
    exact = REL_BUCKETS // 2
    nf = jnp.maximum(n, exact).astype(F32)
    large = exact + (jnp.log(nf / exact) / math.log(REL_MAX_DIST / exact)
                     * (REL_BUCKETS - exact)).astype(jnp.int32)
    large = jnp.minimum(large, REL_BUCKETS - 1)
    return jnp.where(n < exact, n, large)


def _swa_bias_kernel(bucket_ref, rel_bias_ref, out_ref):
    h = pl.program_id(0)
    bucket = bucket_ref[...]
    bias = jnp.zeros(bucket.shape, F32)
    for b in range(REL_BUCKETS):
        bias = jnp.where(bucket == b, rel_bias_ref[b, h], bias)
    i = lax.broadcasted_iota(jnp.int32, bucket.shape, 0)
    j = lax.broadcasted_iota(jnp.int32, bucket.shape, 1)
    rel = BLOCK + i - j
    in_window = (rel >= 0) & (rel < WINDOW)
    out_ref[0, 0] = jnp.where(in_window & (j >= BLOCK), bias, MASK_VALUE)
    out_ref[1, 0] = jnp.where(in_window, bias, MASK_VALUE)


def _swa_bias_table(rel_bias):
    i = jnp.arange(BLOCK)[:, None]
    j = jnp.arange(2 * BLOCK)[None, :]
    bucket = _t5_bucket(BLOCK + i - j).astype(jnp.int32)
    return pl.pallas_call(
        _swa_bias_kernel,
        grid=(SWA_Q_HEADS,),
        in_specs=[pl.BlockSpec((BLOCK, 2 * BLOCK), lambda h: (0, 0)),
                  pl.BlockSpec(memory_space=pltpu.SMEM)],
        out_specs=pl.BlockSpec((2, 1, BLOCK, 2 * BLOCK), lambda h: (0, h, 0, 0)),
        out_shape=jax.ShapeDtypeStruct((2, SWA_Q_HEADS, BLOCK, 2 * BLOCK), F32),
        compiler_params=_params(("parallel",)),
        name="swa_bias_table",
    )(bucket, rel_bias)


def _in_proj_kernel(x_ref, g_ref, w_ref, y_ref, h_ref):
    @pl.when(pl.program_id(1) == 0)
    def _():
        h_ref[...] = _rms(x_ref[...], g_ref[...], D_MODEL).astype(BF16)

    y_ref[...] = _dot(h_ref[...], w_ref[...]).astype(BF16)


def _in_proj(x2, gain, w):
    t = x2.shape[0]
    tm, tn = 1024, IN_TILE_N
    return pl.pallas_call(
        _in_proj_kernel,
        grid=(t // tm, IN_W_PAD // tn),
        in_specs=[pl.BlockSpec((tm, D_MODEL), lambda i, j: (i, 0)),
                  pl.BlockSpec((1, D_MODEL), lambda i, j: (0, 0)),
                  pl.BlockSpec((D_MODEL, tn), lambda i, j: (0, j))],
        out_specs=pl.BlockSpec((tm, tn), lambda i, j: (i, j)),
        out_shape=jax.ShapeDtypeStruct((t, IN_W_PAD), BF16),
        scratch_shapes=[pltpu.VMEM((tm, D_MODEL), BF16)],
        compiler_params=_params(("parallel", "arbitrary")),
        name="in_proj",
    )(x2, gain, w)


def _swa_kernel(q_ref, kp_ref, kc_ref, vp_ref, vc_ref, bias_ref, qg_ref, kg_ref, sink_ref, o_ref):
    scale = HEAD_DIM ** -0.5
    group = SWA_Q_HEADS // SWA_KV_HEADS
    for hk in range(SWA_KV_HEADS):
        cols = slice(hk * HEAD_DIM, (hk + 1) * HEAD_DIM)
        k = jnp.concatenate([kp_ref[:, cols], kc_ref[:, cols]], axis=0).astype(F32)
        k = _rms(k, kg_ref[...], HEAD_DIM).astype(BF16)
        v = jnp.concatenate([vp_ref[:, cols], vc_ref[:, cols]], axis=0)
        for g in range(group):
            h = hk * group + g
            hcols = slice(h * HEAD_DIM, (h + 1) * HEAD_DIM)
            q = _rms(q_ref[:, hcols].astype(F32), qg_ref[...], HEAD_DIM).astype(BF16)
            s = _dot_t(q, k) * scale + bias_ref[0, h]
            sink = sink_ref[0, h]
            m = jnp.maximum(jnp.max(s, axis=-1, keepdims=True), sink)
            p = jnp.exp(s - m)
            denom = jnp.sum(p, axis=-1, keepdims=True) + jnp.exp(sink - m)
            o = _dot(p.astype(BF16), v) / denom
            o_ref[:, hcols] = o.astype(BF16)


def _swa_attention(y, bias, q_gain, k_gain, sinks, batch, seq):
    nb = seq // BLOCK
    kcol = COL_KA // KVA
    vcol = COL_VA // KVA
    row = lambda b, n: b * nb + n
    prev = lambda b, n: b * nb + jnp.maximum(n - 1, 0)
    return pl.pallas_call(
        _swa_kernel,
        grid=(batch, nb),
        in_specs=[pl.BlockSpec((BLOCK, WA), lambda b, n: (row(b, n), COL_QA // WA)),
                  pl.BlockSpec((BLOCK, KVA), lambda b, n: (prev(b, n), kcol)),
                  pl.BlockSpec((BLOCK, KVA), lambda b, n: (row(b, n), kcol)),
                  pl.BlockSpec((BLOCK, KVA), lambda b, n: (prev(b, n), vcol)),
                  pl.BlockSpec((BLOCK, KVA), lambda b, n: (row(b, n), vcol)),
                  pl.BlockSpec((1, SWA_Q_HEADS, BLOCK, 2 * BLOCK),
                               lambda b, n: (jnp.minimum(n, 1), 0, 0, 0)),
                  pl.BlockSpec((1, HEAD_DIM), lambda b, n: (0, 0)),
                  pl.BlockSpec((1, HEAD_DIM), lambda b, n: (0, 0)),
                  pl.BlockSpec(memory_space=pltpu.SMEM)],
        out_specs=pl.BlockSpec((BLOCK, WA), lambda b, n: (row(b, n), 0)),
        out_shape=jax.ShapeDtypeStruct((batch * seq, WA), BF16),
        compiler_params=_params(("parallel", "arbitrary")),
        name="swa_attention",
    )(y, y, y, y, y, bias, q_gain, k_gain, sinks)


def _sb_kernel(q_ref, k_ref, v_ref, o_ref):
    i = pl.program_id(2)
    scale = HEAD_DIM ** -0.5
    q = q_ref[...]
    r = lax.broadcasted_iota(jnp.int32, (BLOCK, BLOCK), 0)
    c = lax.broadcasted_iota(jnp.int32, (BLOCK, BLOCK), 1)
    later = jnp.where(r > c, 1.0, 0.0).astype(BF16)
    earlier = c < r

    def block(kb, carry, diagonal):
        run, acc = carry
        rows = pl.ds(pl.multiple_of(kb * BLOCK, BLOCK), BLOCK)
        z = _dot_t(q, k_ref[rows, :]) * scale
        softplus = jnp.maximum(z, 0.0) + jnp.log1p(jnp.exp(-jnp.abs(z)))
        log_keep = -softplus
        if diagonal:
            log_keep = jnp.where(earlier, log_keep, 0.0)
        hi = log_keep.astype(BF16)
        lo = (log_keep - hi.astype(F32)).astype(BF16)
        between = _dot(hi, later) + _dot(lo, later)
        w = jnp.exp(z - softplus + between + run)
        if diagonal:
            w = jnp.where(earlier, w, 0.0)
        acc = acc + _dot(w.astype(BF16), v_ref[rows, :])
        run = run + jnp.sum(log_keep, axis=-1, keepdims=True)
        return run, acc

    carry = (jnp.zeros((BLOCK, 1), F32), jnp.zeros((BLOCK, HEAD_DIM), F32))
    carry = block(i, carry, True)
    carry = lax.fori_loop(0, i, lambda t, cr: block(i - 1 - t, cr, False), carry)
    o_ref[...] = carry[1].astype(BF16)


def _sb_attention(y, batch, seq):
    nq = seq // BLOCK
    qc, kc, vc = COL_QB // HEAD_DIM, COL_KB // HEAD_DIM, COL_VB // HEAD_DIM
    return pl.pallas_call(
        _sb_kernel,
        grid=(batch, SB_HEADS, nq),
        in_specs=[pl.BlockSpec((BLOCK, HEAD_DIM), lambda b, h, i: (b * nq + i, qc + h)),
                  pl.BlockSpec((seq, HEAD_DIM), lambda b, h, i: (b, kc + h)),
                  pl.BlockSpec((seq, HEAD_DIM), lambda b, h, i: (b, vc + h))],
        out_specs=pl.BlockSpec((BLOCK, HEAD_DIM), lambda b, h, i: (b * nq + i, h)),
        out_shape=jax.ShapeDtypeStruct((batch * seq, WB), BF16),
        compiler_params=_params(("parallel", "parallel", "arbitrary")),
        name="sb_attention",
    )(y, y, y)


def _rope(x, cos, sin_signed):
    lane = lax.broadcasted_iota(jnp.int32, x.shape, 1)
    first_half = (lane & (MLA_ROPE - 1)) < (MLA_ROPE // 2)
    partner = jnp.where(first_half,
                        pltpu.roll(x, HEAD_DIM - MLA_ROPE // 2, 1),
                        pltpu.roll(x, MLA_ROPE // 2, 1))
    return x * cos + partner * sin_signed


def _mla_prep_kernel(cq_ref, ckv_ref, kr_ref, cos_ref, sin_ref, wuq_ref, wukv_ref,
                     cqg_ref, ckvg_ref, qgn_ref, qgr_ref, kgn_ref, kgr_ref,
                     q_ref, k_ref, v_ref):
    scale = MLA_QK ** -0.5
    cos, sin = cos_ref[...], sin_ref[...]
    cqn = _rms(cq_ref[...].astype(F32), cqg_ref[...], MLA_Q_RANK).astype(BF16)
    qc = _dot(cqn, wuq_ref[...])
    ckvn = _rms(ckv_ref[...].astype(F32), ckvg_ref[...], MLA_KV_RANK).astype(BF16)
    kvc = _dot(ckvn, wukv_ref[...])
    k_rope = _rope(_rms(kr_ref[...].astype(F32), kgr_ref[...], MLA_ROPE), cos, sin).astype(BF16)
    rope0 = MLA_HEADS * MLA_NOPE
    for h in range(MLA_HEADS):
        nope = slice(h * MLA_NOPE, (h + 1) * MLA_NOPE)
        padded_rope = slice(rope0 + h * HEAD_DIM, rope0 + (h + 1) * HEAD_DIM)
        qn = _rms(qc[:, nope], qgn_ref[...], MLA_NOPE) * scale
        qr = _rope(_rms(qc[:, padded_rope], qgr_ref[...], MLA_ROPE), cos, sin) * scale
        q_ref[:, 2 * h * HEAD_DIM:(2 * h + 1) * HEAD_DIM] = qn.astype(BF16)
        q_ref[:, (2 * h + 1) * HEAD_DIM:(2 * h + 2) * HEAD_DIM] = qr.astype(BF16)
        kn = _rms(kvc[:, nope], kgn_ref[...], MLA_NOPE)
        k_ref[:, 2 * h * HEAD_DIM:(2 * h + 1) * HEAD_DIM] = kn.astype(BF16)
        k_ref[:, (2 * h + 1) * HEAD_DIM:(2 * h + 2) * HEAD_DIM] = k_rope
    v_ref[...] = kvc[:, rope0:].astype(BF16)


def _mla_prep(y, cos_t, sin_t, w_uq, w_ukv, cq_gain, ckv_gain, q_gain, k_gain):
    t = y.shape[0]
    tm = 512
    pad = jnp.zeros((HEAD_DIM - MLA_ROPE,), F32)
    row = lambda v: v.reshape(1, -1)
    gains = [row(cq_gain), row(ckv_gain),
             row(q_gain[:MLA_NOPE]), row(jnp.concatenate([q_gain[MLA_NOPE:], pad])),
             row(k_gain[:MLA_NOPE]), row(jnp.concatenate([k_gain[MLA_NOPE:], pad]))]
    wq = 2 * MLA_HEADS * HEAD_DIM
    return pl.pallas_call(
        _mla_prep_kernel,
        grid=(t // tm,),
        in_specs=[pl.BlockSpec((tm, MLA_Q_RANK), lambda i: (i, COL_CQ // MLA_Q_RANK)),
                  pl.BlockSpec((tm, MLA_KV_RANK), lambda i: (i, COL_CKV // MLA_KV_RANK)),
                  pl.BlockSpec((tm, HEAD_DIM), lambda i: (i, COL_KR // HEAD_DIM)),
                  pl.BlockSpec((tm, HEAD_DIM), lambda i: (i, 0)),
                  pl.BlockSpec((tm, HEAD_DIM), lambda i: (i, 0)),
                  _resident(w_uq.shape), _resident(w_ukv.shape)]
                 + [_resident(g.shape) for g in gains],
        out_specs=[pl.BlockSpec((tm, wq), lambda i: (i, 0)),
                   pl.BlockSpec((tm, wq), lambda i: (i, 0)),
                   pl.BlockSpec((tm, WC), lambda i: (i, 0))],
        out_shape=[jax.ShapeDtypeStruct((t, wq), BF16),
                   jax.ShapeDtypeStruct((t, wq), BF16),
                   jax.ShapeDtypeStruct((t, WC), BF16)],
        compiler_params=_params(("parallel",)),
        name="mla_prep",
    )(y, y, y, cos_t, sin_t, w_uq, w_ukv, *gains)


def _mla_kernel(q_ref, k_ref, v_ref, o_ref):
    i = pl.program_id(2)
    q = q_ref[...]
    r = lax.broadcasted_iota(jnp.int32, (BLOCK, BLOCK), 0)
    c = lax.broadcasted_iota(jnp.int32, (BLOCK, BLOCK), 1)
    causal = c <= r

    def block(kb, carry, diagonal):
        m, l, acc = carry
        rows = pl.ds(pl.multiple_of(kb * BLOCK, BLOCK), BLOCK)
        s = _dot_t(q, k_ref[rows, :])
        if diagonal:
            s = jnp.where(causal, s, MASK_VALUE)
        m_new = jnp.maximum(m, jnp.max(s, axis=-1, keepdims=True))
        alpha = jnp.exp(m - m_new)
        p = jnp.exp(s - m_new)
        l = alpha * l + jnp.sum(p, axis=-1, keepdims=True)
        acc = alpha * acc + _dot(p.astype(BF16), v_ref[rows, :])
        return m_new, l, acc

    carry = (jnp.full((BLOCK, 1), MASK_VALUE, F32), jnp.zeros((BLOCK, 1), F32),
             jnp.zeros((BLOCK, MLA_V), F32))
    carry = block(i, carry, True)
    carry = lax.fori_loop(0, i, lambda t, cr: block(i - 1 - t, cr, False), carry)
    o_ref[...] = (carry[2] / carry[1]).astype(BF16)


def _mla_attention(qm, km, vm, batch, seq):
    nq = seq // BLOCK
    wqk = 2 * HEAD_DIM
    return pl.pallas_call(
        _mla_kernel,
        grid=(batch, MLA_HEADS, nq),
        in_specs=[pl.BlockSpec((BLOCK, wqk), lambda b, h, i: (b * nq + i, h)),
                  pl.BlockSpec((seq, wqk), lambda b, h, i: (b, h)),
                  pl.BlockSpec((seq, MLA_V), lambda b, h, i: (b, h))],
        out_specs=pl.BlockSpec((BLOCK, MLA_V), lambda b, h, i: (b * nq + i, h)),
        out_shape=jax.ShapeDtypeStruct((batch * seq, WC), BF16),
        compiler_params=_params(("parallel", "parallel", "arbitrary")),
        name="mla_attention",
    )(qm, km, vm)


def _merge_kernel(oa_ref, ob_ref, oc_ref, ga_ref, gb_ref, gc_ref, x_ref, wbr_ref, wout_ref, o_ref):
    ma = _dot(oa_ref[...], wbr_ref[0:WA, :])
    mb = _dot(ob_ref[...], wbr_ref[WA:WA + WB, :])
    mc = _dot(oc_ref[...], wbr_ref[WA + WB:, :])
    merged = (jax.nn.sigmoid(ga_ref[...].astype(F32)) * ma
              + jax.nn.sigmoid(gb_ref[...].astype(F32)) * mb
              + jax.nn.sigmoid(gc_ref[...].astype(F32)) * mc)
    o_ref[...] = x_ref[...] + _dot(merged.astype(BF16), wout_ref[...])


def _merge(o_a, o_b, o_c, y, x2, w_branch, w_out):
    t = x2.shape[0]
    tm = 256
    gcol = COL_GATES // D_MODEL
    return pl.pallas_call(
        _merge_kernel,
        grid=(t // tm,),
        in_specs=[pl.BlockSpec((tm, WA), lambda i: (i, 0)),
                  pl.BlockSpec((tm, WB), lambda i: (i, 0)),
                  pl.BlockSpec((tm, WC), lambda i: (i, 0)),
                  pl.BlockSpec((tm, D_MODEL), lambda i: (i, gcol)),
                  pl.BlockSpec((tm, D_MODEL), lambda i: (i, gcol + 1)),
                  pl.BlockSpec((tm, D_MODEL), lambda i: (i, gcol + 2)),
                  pl.BlockSpec((tm, D_MODEL), lambda i: (i, 0)),
                  _resident(w_branch.shape), _resident(w_out.shape)],
        out_specs=pl.BlockSpec((tm, D_MODEL), lambda i: (i, 0)),
        out_shape=jax.ShapeDtypeStruct((t, D_MODEL), F32),
        compiler_params=_params(("parallel",)),
        name="gated_merge",
    )(o_a, o_b, o_c, y, y, y, x2, w_branch, w_out)


def _mem_kv_kernel(mem_ref, g_ref, wkv_ref, kg_ref, k_ref, v_ref):
    memn = _rms(mem_ref[...], g_ref[...], D_MODEL).astype(BF16)
    kv = _dot(memn, wkv_ref[...])
    for h in range(XA_HEADS):
        cols = slice(h * HEAD_DIM, (h + 1) * HEAD_DIM)
        k_ref[:, cols] = _rms(kv[:, cols], kg_ref[...], HEAD_DIM).astype(BF16)
    v_ref[...] = kv[:, XA_W:].astype(BF16)


def _mem_kv(mem2, gain, w_kv, k_gain, batch, mem_len):
    return pl.pallas_call(
        _mem_kv_kernel,
        grid=(batch,),
        in_specs=[pl.BlockSpec((mem_len, D_MODEL), lambda b: (b, 0)),
                  _resident(gain.shape), _resident(w_kv.shape), _resident(k_gain.shape)],
        out_specs=[pl.BlockSpec((mem_len, XA_W), lambda b: (b, 0)),
                   pl.BlockSpec((mem_len, XA_W), lambda b: (b, 0))],
        out_shape=[jax.ShapeDtypeStruct((batch * mem_len, XA_W), BF16)] * 2,
        compiler_params=_params(("parallel",)),
        name="mem_kv",
    )(mem2, gain, w_kv, k_gain)


def _xattn_kernel(x_ref, g_ref, wq_ref, qg_ref, k_ref, v_ref, wo_ref, o_ref):
    scale = HEAD_DIM ** -0.5
    x = x_ref[...]
    xn = _rms(x, g_ref[...], D_MODEL).astype(BF16)
    q = _dot(xn, wq_ref[...])
    heads = []
    for h in range(XA_HEADS):
        cols = slice(h * HEAD_DIM, (h + 1) * HEAD_DIM)
        qh = (_rms(q[:, cols], qg_ref[...], HEAD_DIM) * scale).astype(BF16)
        s = _dot_t(qh, k_ref[:, cols])
        p = jnp.exp(s - jnp.max(s, axis=-1, keepdims=True))
        oh = _dot(p.astype(BF16), v_ref[:, cols]) / jnp.sum(p, axis=-1, keepdims=True)
        heads.append(oh.astype(BF16))
    o = jnp.concatenate(heads, axis=-1)
    o_ref[...] = x + _dot(o, wo_ref[...])


def _xattn(x2, gain, w_q, q_gain, k_mem, v_mem, w_o, seq, mem_len):
    t = x2.shape[0]
    tm = 512
    per_batch = seq // tm
    return pl.pallas_call(
        _xattn_kernel,
        grid=(t // tm,),
        in_specs=[pl.BlockSpec((tm, D_MODEL), lambda i: (i, 0)),
                  _resident(gain.shape), _resident(w_q.shape), _resident(q_gain.shape),
                  pl.BlockSpec((mem_len, XA_W), lambda i: (i // per_batch, 0)),
                  pl.BlockSpec((mem_len, XA_W), lambda i: (i // per_batch, 0)),
                  _resident(w_o.shape)],
        out_specs=pl.BlockSpec((tm, D_MODEL), lambda i: (i, 0)),
        out_shape=jax.ShapeDtypeStruct((t, D_MODEL), F32),
        compiler_params=_params(("parallel",)),
        name="mem_xattn",
    )(x2, gain, w_q, q_gain, k_mem, v_mem, w_o)


def _ffn_kernel(x_ref, g_ref, wg_ref, wu_ref, wd_ref, o_ref, h_ref):
    @pl.when(pl.program_id(1) == 0)
    def _():
        x = x_ref[...]
        h_ref[...] = _rms(x, g_ref[...], D_MODEL).astype(BF16)
        o_ref[...] = x

    h = h_ref[...]
    gate = _dot(h, wg_ref[...])
    up = _dot(h, wu_ref[...])
    act = (gate * jax.nn.sigmoid(gate) * up).astype(BF16)
    o_ref[...] += _dot(act, wd_ref[...])


def _ffn(x2, gain, w_gu, w_down):
    t = x2.shape[0]
    tm, tf = 512, 512
    nf = D_FF // tf
    return pl.pallas_call(
        _ffn_kernel,
        grid=(t // tm, nf),
        in_specs=[pl.BlockSpec((tm, D_MODEL), lambda i, f: (i, 0)),
                  pl.BlockSpec((1, D_MODEL), lambda i, f: (0, 0)),
                  pl.BlockSpec((D_MODEL, tf), lambda i, f: (0, f)),
                  pl.BlockSpec((D_MODEL, tf), lambda i, f: (0, nf + f)),
                  pl.BlockSpec((tf, D_MODEL), lambda i, f: (f, 0))],
        out_specs=pl.BlockSpec((tm, D_MODEL), lambda i, f: (i, 0)),
        out_shape=jax.ShapeDtypeStruct((t, D_MODEL), F32),
        scratch_shapes=[pltpu.VMEM((tm, D_MODEL), BF16)],
        compiler_params=_params(("parallel", "arbitrary")),
        name="swiglu_ffn",
    )(x2, gain, w_gu, w_gu, w_down)


def _in_proj_weight(w):
    kr0 = COL_GATES
    main, kr, gates = w[:, :kr0], w[:, kr0:kr0 + MLA_ROPE], w[:, kr0 + MLA_ROPE:]
    pad = jnp.zeros((w.shape[0], IN_W_PAD - COL_KR - MLA_ROPE), w.dtype)
    return jnp.concatenate([main, gates, kr, pad], axis=1).astype(BF16)


def _mla_uq_weight(w):
    r = w.shape[0]
    w = w.reshape(r, MLA_HEADS, MLA_QK)
    nope = w[:, :, :MLA_NOPE].reshape(r, MLA_HEADS * MLA_NOPE)
    rope = jnp.pad(w[:, :, MLA_NOPE:], ((0, 0), (0, 0), (0, HEAD_DIM - MLA_ROPE)))
    return jnp.concatenate([nope, rope.reshape(r, MLA_HEADS * HEAD_DIM)], axis=1).astype(BF16)


def _split_heads_weight(w, heads, first):
    k = w.shape[0]
    w = w.reshape(k, heads, -1)
    a = w[:, :, :first].reshape(k, -1)
    b = w[:, :, first:].reshape(k, -1)
    return jnp.concatenate([a, b], axis=1).astype(BF16)


def kernel(x, mem, positions, rel_bias, norm_mix, w_in, swa_q_norm, swa_k_norm, swa_sinks,
           mla_cq_norm, mla_ckv_norm, mla_w_uq, mla_w_ukv, mla_q_norm, mla_k_norm,
           w_branch, w_out, norm_xa, norm_mem, xa_wq, xa_wkv, xa_q_norm, xa_k_norm, xa_wo,
           norm_ffn, ffn_w_gu, ffn_w_down):
    batch, seq, d = x.shape
    mem_len = mem.shape[1]
    depth = w_in.shape[0]
    assert d == D_MODEL and seq % 512 == 0 and (batch * seq) % 1024 == 0 and mem_len % 8 == 0
    row = lambda v: v.reshape(1, -1)

    x2 = x.reshape(batch * seq, d)
    mem2 = mem.reshape(batch * mem_len, d)
    cos_t, sin_t = _rope_tables(positions)
    swa_bias = _swa_bias_table(rel_bias)

    for l in range(depth):
        y = _in_proj(x2, row(norm_mix[l]), _in_proj_weight(w_in[l]))
        o_a = _swa_attention(y, swa_bias, row(swa_q_norm[l]), row(swa_k_norm[l]),
                             row(swa_sinks[l]), batch, seq)
        o_b = _sb_attention(y, batch, seq)
        qm, km, vm = _mla_prep(y, cos_t, sin_t, _mla_uq_weight(mla_w_uq[l]),
                               _split_heads_weight(mla_w_ukv[l], MLA_HEADS, MLA_NOPE),
                               mla_cq_norm[l], mla_ckv_norm[l], mla_q_norm[l], mla_k_norm[l])
        o_c = _mla_attention(qm, km, vm, batch, seq)
        x2 = _merge(o_a, o_b, o_c, y, x2, w_branch[l].astype(BF16), w_out[l].astype(BF16))

        k_mem, v_mem = _mem_kv(mem2, row(norm_mem[l]),
                               _split_heads_weight(xa_wkv[l], XA_HEADS, HEAD_DIM),
                               row(xa_k_norm[l]), batch, mem_len)
        x2 = _xattn(x2, row(norm_xa[l]), xa_wq[l].astype(BF16), row(xa_q_norm[l]),
                    k_mem, v_mem, xa_wo[l].astype(BF16), seq, mem_len)
        x2 = _ffn(x2, row(norm_ffn[l]), ffn_w_gu[l].astype(BF16), ffn_w_down[l].astype(BF16))
    return x2.reshape(batch, seq, d)
```

```python
import functools
import math

import jax
import jax.numpy as jnp
from jax import lax
from jax.experimental import pallas as pl
from jax.experimental.pallas import tpu as pltpu

F32 = jnp.float32
BF16 = jnp.bfloat16

D_MODEL = 2048
HEAD_DIM = 128
BLOCK = 128
SWA_Q_HEADS = 8
SWA_KV_HEADS = 2
WINDOW = 128
SB_HEADS = 4
MLA_HEADS = 4
MLA_Q_RANK = 512
MLA_KV_RANK = 512
MLA_NOPE = 128
MLA_ROPE = 64
MLA_V = 128
ROPE_THETA = 10000.0
XA_HEADS = 4
REL_BUCKETS = 32
REL_MAX_DIST = 128
D_FF = ((8 * D_MODEL + 3 * 256 - 1) // (3 * 256)) * 256
EPS = 1e-6

WA = SWA_Q_HEADS * HEAD_DIM
KVA = SWA_KV_HEADS * HEAD_DIM
WB = SB_HEADS * HEAD_DIM
WC = MLA_HEADS * MLA_V
XA_W = XA_HEADS * HEAD_DIM
MLA_QK = MLA_NOPE + MLA_ROPE

COL_QA = 0
COL_KA = COL_QA + WA
COL_VA = COL_KA + KVA
COL_QB = COL_VA + KVA
COL_KB = COL_QB + WB
COL_VB = COL_KB + WB
COL_CQ = COL_VB + WB
COL_CKV = COL_CQ + MLA_Q_RANK
COL_GATES = COL_CKV + MLA_KV_RANK
COL_KR = COL_GATES + 3 * D_MODEL
IN_TILE_N = 768
IN_W_PAD = -(-(COL_KR + HEAD_DIM) // IN_TILE_N) * IN_TILE_N

SWA_BLOCKS_PER_STEP = 4
ATT_TILE = 256
MASK_VALUE = -1e30
V7X_VMEM_LIMIT_BYTES = 56 * 1024 * 1024


def _params(semantics):
    return pltpu.CompilerParams(dimension_semantics=semantics,
                                vmem_limit_bytes=V7X_VMEM_LIMIT_BYTES)


def _rms(x, gain, width):
    ms = jnp.sum(x * x, axis=-1, keepdims=True) * (1.0 / width)
    return x * lax.rsqrt(ms + EPS) * gain


def _dot(a, b):
    return jnp.dot(a, b, preferred_element_type=F32)


def _dot_t(a, b):
    return lax.dot_general(a, b, (((1,), (1,)), ((), ())), preferred_element_type=F32)


def _resident(shape):
    nd = len(shape)
    return pl.BlockSpec(shape, lambda *_: (0,) * nd, pipeline_mode=pl.Buffered(1))


def _rope_table_kernel(pos_ref, inv_ref, cos_ref, sin_ref):
    ang = pos_ref[...].astype(F32) * inv_ref[...]
    lane = lax.broadcasted_iota(jnp.int32, ang.shape, 1)
    first_half = (lane & (MLA_ROPE - 1)) < (MLA_ROPE // 2)
    cos_ref[...] = jnp.cos(ang)
    s = jnp.sin(ang)
    sin_ref[...] = jnp.where(first_half, -s, s)


def _rope_tables(positions):
    t = positions.size
    half = MLA_ROPE // 2
    inv = ROPE_THETA ** (-jnp.arange(half, dtype=F32) / half)
    inv_l = jnp.tile(inv, HEAD_DIM // half).reshape(1, HEAD_DIM)
    pos_l = jnp.broadcast_to(positions.reshape(t, 1), (t, HEAD_DIM))
    tm = 1024
    return pl.pallas_call(
        _rope_table_kernel,
        grid=(t // tm,),
        in_specs=[pl.BlockSpec((tm, HEAD_DIM), lambda i: (i, 0)),
                  pl.BlockSpec((1, HEAD_DIM), lambda i: (0, 0))],
        out_specs=[pl.BlockSpec((tm, HEAD_DIM), lambda i: (i, 0)),
                   pl.BlockSpec((tm, HEAD_DIM), lambda i: (i, 0))],
        out_shape=[jax.ShapeDtypeStruct((t, HEAD_DIM), F32)] * 2,
        compiler_params=_params(("parallel",)),
        name="rope_tables",
    )(pos_l, inv_l)


def _t5_bucket(rel):
    n = jnp.maximum(rel, 0)
    exact = REL_BUCKETS // 2
    nf = jnp.maximum(n, exact).astype(F32)
    large = exact + (jnp.log(nf / exact) / math.log(REL_MAX_DIST / exact)
                     * (REL_BUCKETS - exact)).astype(jnp.int32)
    large = jnp.minimum(large, REL_BUCKETS - 1)
    return jnp.where(n < exact, n, large)


def _swa_bias_kernel(bucket_ref, rel_bias_ref, out_ref):
    h = pl.program_id(0)
    bucket = bucket_ref[...]
    bias = jnp.zeros(bucket.shape, F32)
    for b in range(REL_BUCKETS):
        bias = jnp.where(bucket == b, rel_bias_ref[b, h], bias)
    i = lax.broadcasted_iota(jnp.int32, bucket.shape, 0)
    j = lax.broadcasted_iota(jnp.int32, bucket.shape, 1)
    rel = BLOCK + i - j
    in_window = (rel >= 0) & (rel < WINDOW)
    out_ref[0, 0] = jnp.where(in_window & (j >= BLOCK), bias, MASK_VALUE)
    out_ref[1, 0] = jnp.where(in_window, bias, MASK_VALUE)


def _swa_bias_table(rel_bias):
    i = jnp.arange(BLOCK)[:, None]
    j = jnp.arange(2 * BLOCK)[None, :]
    bucket = _t5_bucket(BLOCK + i - j).astype(jnp.int32)
    return pl.pallas_call(
        _swa_bias_kernel,
        grid=(SWA_Q_HEADS,),
        in_specs=[pl.BlockSpec((BLOCK, 2 * BLOCK), lambda h: (0, 0)),
                  pl.BlockSpec(memory_space=pltpu.SMEM)],
        out_specs=pl.BlockSpec((2, 1, BLOCK, 2 * BLOCK), lambda h: (0, h, 0, 0)),
        out_shape=jax.ShapeDtypeStruct((2, SWA_Q_HEADS, BLOCK, 2 * BLOCK), F32),
        compiler_params=_params(("parallel",)),
        name="swa_bias_table",
    )(bucket, rel_bias)


def _in_proj_kernel(x_ref, g_ref, w_ref, y_ref, h_ref):
    @pl.when(pl.program_id(1) == 0)
    def _():
        h_ref[...] = _rms(x_ref[...], g_ref[...], D_MODEL).astype(BF16)

    y_ref[...] = _dot(h_ref[...], w_ref[...]).astype(BF16)


def _in_proj(x2, gain, w):
    t = x2.shape[0]
    tm, tn = 1024, IN_TILE_N
    return pl.pallas_call(
        _in_proj_kernel,
        grid=(t // tm, IN_W_PAD // tn),
        in_specs=[pl.BlockSpec((tm, D_MODEL), lambda i, j: (i, 0)),
                  pl.BlockSpec((1, D_MODEL), lambda i, j: (0, 0)),
                  pl.BlockSpec((D_MODEL, tn), lambda i, j: (0, j))],
        out_specs=pl.BlockSpec((tm, tn), lambda i, j: (i, j)),
        out_shape=jax.ShapeDtypeStruct((t, IN_W_PAD), BF16),
        scratch_shapes=[pltpu.VMEM((tm, D_MODEL), BF16)],
        compiler_params=_params(("parallel", "arbitrary")),
        name="in_proj",
    )(x2, gain, w)


def _swa_kernel(q_ref, kp_ref, kc_ref, vp_ref, vc_ref, bias_ref, qg_ref, kg_ref, sink_ref, o_ref):
    scale = HEAD_DIM ** -0.5
    group = SWA_Q_HEADS // SWA_KV_HEADS
    first_variant = jnp.minimum(pl.program_id(1), 1)
    for hk in range(SWA_KV_HEADS):
        cols = slice(hk * HEAD_DIM, (hk + 1) * HEAD_DIM)
        k_all = jnp.concatenate([kp_ref[:, cols], kc_ref[:, cols]], axis=0).astype(F32)
        k_all = _rms(k_all, kg_ref[...], HEAD_DIM).astype(BF16)
        v_all = jnp.concatenate([vp_ref[:, cols], vc_ref[:, cols]], axis=0)
        for blk in range(SWA_BLOCKS_PER_STEP):
            rows = slice(blk * BLOCK, (blk + 1) * BLOCK)
            band = slice(blk * BLOCK, (blk + 2) * BLOCK)
            k, v = k_all[band], v_all[band]
            for g in range(group):
                h = hk * group + g
                hcols = slice(h * HEAD_DIM, (h + 1) * HEAD_DIM)
                q = _rms(q_ref[rows, hcols].astype(F32), qg_ref[...], HEAD_DIM).astype(BF16)
                bias = bias_ref[first_variant, h] if blk == 0 else bias_ref[1, h]
                s = _dot_t(q, k) * scale + bias
                sink = sink_ref[0, h]
                m = jnp.maximum(jnp.max(s, axis=-1, keepdims=True), sink)
                p = jnp.exp(s - m)
                denom = jnp.sum(p, axis=-1, keepdims=True) + jnp.exp(sink - m)
                o = _dot(p.astype(BF16), v) / denom
                o_ref[rows, hcols] = o.astype(BF16)


def _swa_attention(y, bias, q_gain, k_gain, sinks, batch, seq):
    span = SWA_BLOCKS_PER_STEP * BLOCK
    steps = seq // span
    kcol = COL_KA // KVA
    vcol = COL_VA // KVA
    row = lambda b, n: b * steps + n
    prev = lambda b, n: (b * steps + n) * SWA_BLOCKS_PER_STEP - jnp.minimum(n, 1)
    return pl.pallas_call(
        _swa_kernel,
        grid=(batch, steps),
        in_specs=[pl.BlockSpec((span, WA), lambda b, n: (row(b, n), COL_QA // WA)),
                  pl.BlockSpec((BLOCK, KVA), lambda b, n: (prev(b, n), kcol)),
                  pl.BlockSpec((span, KVA), lambda b, n: (row(b, n), kcol)),
                  pl.BlockSpec((BLOCK, KVA), lambda b, n: (prev(b, n), vcol)),
                  pl.BlockSpec((span, KVA), lambda b, n: (row(b, n), vcol)),
                  _resident(bias.shape),
                  pl.BlockSpec((1, HEAD_DIM), lambda b, n: (0, 0)),
                  pl.BlockSpec((1, HEAD_DIM), lambda b, n: (0, 0)),
                  pl.BlockSpec(memory_space=pltpu.SMEM)],
        out_specs=pl.BlockSpec((span, WA), lambda b, n: (row(b, n), 0)),
        out_shape=jax.ShapeDtypeStruct((batch * seq, WA), BF16),
        compiler_params=_params(("parallel", "arbitrary")),
        name="swa_attention",
    )(y, y, y, y, y, bias, q_gain, k_gain, sinks)


def _sb_kernel(q_ref, k_ref, v_ref, o_ref, *, seq):
    scale = HEAD_DIM ** -0.5
    r = lax.broadcasted_iota(jnp.int32, (ATT_TILE, ATT_TILE), 0)
    c = lax.broadcasted_iota(jnp.int32, (ATT_TILE, ATT_TILE), 1)
    later = jnp.where(r > c, 1.0, 0.0).astype(BF16)
    earlier = c < r
    for i in range(seq // ATT_TILE):
        q = q_ref[i * ATT_TILE:(i + 1) * ATT_TILE, :]
        run = jnp.zeros((ATT_TILE, 1), F32)
        acc = jnp.zeros((ATT_TILE, HEAD_DIM), F32)
        for j in range(i, -1, -1):
            rows = slice(j * ATT_TILE, (j + 1) * ATT_TILE)
            z = _dot_t(q, k_ref[rows, :]) * scale
            softplus = jnp.maximum(z, 0.0) + jnp.log(1.0 + jnp.exp(-jnp.abs(z)))
            log_keep = -softplus
            if j == i:
                log_keep = jnp.where(earlier, log_keep, 0.0)
            hi = log_keep.astype(BF16)
            lo = (log_keep - hi.astype(F32)).astype(BF16)
            split = _dot(jnp.concatenate([hi, lo], axis=0), later)
            between = split[:ATT_TILE] + split[ATT_TILE:]
            w = jnp.exp(z - softplus + between + run)
            if j == i:
                w = jnp.where(earlier, w, 0.0)
            acc = acc + _dot(w.astype(BF16), v_ref[rows, :])
            if j > 0:
                run = run + jnp.sum(log_keep, axis=-1, keepdims=True)
        o_ref[i * ATT_TILE:(i + 1) * ATT_TILE, :] = acc.astype(BF16)


def _sb_attention(y, batch, seq):
    qc, kc, vc = COL_QB // HEAD_DIM, COL_KB // HEAD_DIM, COL_VB // HEAD_DIM
    return pl.pallas_call(
        functools.partial(_sb_kernel, seq=seq),
        grid=(batch, SB_HEADS),
        in_specs=[pl.BlockSpec((seq, HEAD_DIM), lambda b, h: (b, qc + h)),
                  pl.BlockSpec((seq, HEAD_DIM), lambda b, h: (b, kc + h)),
                  pl.BlockSpec((seq, HEAD_DIM), lambda b, h: (b, vc + h))],
        out_specs=pl.BlockSpec((seq, HEAD_DIM), lambda b, h: (b, h)),
        out_shape=jax.ShapeDtypeStruct((batch * seq, WB), BF16),
        compiler_params=_params(("parallel", "parallel")),
        name="sb_attention",
    )(y, y, y)


def _rope(x, cos, sin_signed):
    lane = lax.broadcasted_iota(jnp.int32, x.shape, 1)
    first_half = (lane & (MLA_ROPE - 1)) < (MLA_ROPE // 2)
    partner = jnp.where(first_half,
                        pltpu.roll(x, HEAD_DIM - MLA_ROPE // 2, 1),
                        pltpu.roll(x, MLA_ROPE // 2, 1))
    return x * cos + partner * sin_signed


def _mla_prep_kernel(cq_ref, ckv_ref, kr_ref, cos_ref, sin_ref, wuq_ref, wukv_ref,
                     cqg_ref, ckvg_ref, qgn_ref, qgr_ref, kgn_ref, kgr_ref,
                     q_ref, k_ref, v_ref):
    scale = MLA_QK ** -0.5 * math.log2(math.e)
    cos, sin = cos_ref[...], sin_ref[...]
    cqn = _rms(cq_ref[...].astype(F32), cqg_ref[...], MLA_Q_RANK).astype(BF16)
    qc = _dot(cqn, wuq_ref[...])
    ckvn = _rms(ckv_ref[...].astype(F32), ckvg_ref[...], MLA_KV_RANK).astype(BF16)
    kvc = _dot(ckvn, wukv_ref[...])
    k_rope = _rope(_rms(kr_ref[...].astype(F32), kgr_ref[...], MLA_ROPE), cos, sin).astype(BF16)
    rope0 = MLA_HEADS * MLA_NOPE
    for h in range(MLA_HEADS):
        nope = slice(h * MLA_NOPE, (h + 1) * MLA_NOPE)
        padded_rope = slice(rope0 + h * HEAD_DIM, rope0 + (h + 1) * HEAD_DIM)
        qn = _rms(qc[:, nope], qgn_ref[...], MLA_NOPE) * scale
        qr = _rope(_rms(qc[:, padded_rope], qgr_ref[...], MLA_ROPE), cos, sin) * scale
        q_ref[:, 2 * h * HEAD_DIM:(2 * h + 1) * HEAD_DIM] = qn.astype(BF16)
        q_ref[:, (2 * h + 1) * HEAD_DIM:(2 * h + 2) * HEAD_DIM] = qr.astype(BF16)
        kn = _rms(kvc[:, nope], kgn_ref[...], MLA_NOPE)
        k_ref[:, 2 * h * HEAD_DIM:(2 * h + 1) * HEAD_DIM] = kn.astype(BF16)
        k_ref[:, (2 * h + 1) * HEAD_DIM:(2 * h + 2) * HEAD_DIM] = k_rope
    v_ref[...] = kvc[:, rope0:].astype(BF16)


def _mla_prep(y, cos_t, sin_t, w_uq, w_ukv, cq_gain, ckv_gain, q_gain, k_gain):
    t = y.shape[0]
    tm = 512
    pad = jnp.zeros((HEAD_DIM - MLA_ROPE,), F32)
    row = lambda v: v.reshape(1, -1)
    gains = [row(cq_gain), row(ckv_gain),
             row(q_gain[:MLA_NOPE]), row(jnp.concatenate([q_gain[MLA_NOPE:], pad])),
             row(k_gain[:MLA_NOPE]), row(jnp.concatenate([k_gain[MLA_NOPE:], pad]))]
    wq = 2 * MLA_HEADS * HEAD_DIM
    return pl.pallas_call(
        _mla_prep_kernel,
        grid=(t // tm,),
        in_specs=[pl.BlockSpec((tm, MLA_Q_RANK), lambda i: (i, COL_CQ // MLA_Q_RANK)),
                  pl.BlockSpec((tm, MLA_KV_RANK), lambda i: (i, COL_CKV // MLA_KV_RANK)),
                  pl.BlockSpec((tm, HEAD_DIM), lambda i: (i, COL_KR // HEAD_DIM)),
                  pl.BlockSpec((tm, HEAD_DIM), lambda i: (i, 0)),
                  pl.BlockSpec((tm, HEAD_DIM), lambda i: (i, 0)),
                  _resident(w_uq.shape), _resident(w_ukv.shape)]
                 + [_resident(g.shape) for g in gains],
        out_specs=[pl.BlockSpec((tm, wq), lambda i: (i, 0)),
                   pl.BlockSpec((tm, wq), lambda i: (i, 0)),
                   pl.BlockSpec((tm, WC), lambda i: (i, 0))],
        out_shape=[jax.ShapeDtypeStruct((t, wq), BF16),
                   jax.ShapeDtypeStruct((t, wq), BF16),
                   jax.ShapeDtypeStruct((t, WC), BF16)],
        compiler_params=_params(("parallel",)),
        name="mla_prep",
    )(y, y, y, cos_t, sin_t, w_uq, w_ukv, *gains)


def _mla_kernel(q_ref, k_ref, v_ref, o_ref, *, seq):
    r = lax.broadcasted_iota(jnp.int32, (ATT_TILE, ATT_TILE), 0)
    c = lax.broadcasted_iota(jnp.int32, (ATT_TILE, ATT_TILE), 1)
    causal = c <= r
    for i in range(seq // ATT_TILE):
        q = q_ref[i * ATT_TILE:(i + 1) * ATT_TILE, :]
        s = [_dot_t(q, k_ref[j * ATT_TILE:(j + 1) * ATT_TILE, :]) for j in range(i + 1)]
        s[i] = jnp.where(causal, s[i], MASK_VALUE)
        m = functools.reduce(jnp.maximum, [jnp.max(sj, axis=-1, keepdims=True) for sj in s])
        p = [jnp.exp2(sj - m) for sj in s]
        l = functools.reduce(jnp.add, [jnp.sum(pj, axis=-1, keepdims=True) for pj in p])
        p_all = jnp.concatenate([pj.astype(BF16) for pj in p], axis=1)
        acc = _dot(p_all, v_ref[0:(i + 1) * ATT_TILE, :])
        o_ref[i * ATT_TILE:(i + 1) * ATT_TILE, :] = (acc / l).astype(BF16)


def _mla_attention(qm, km, vm, batch, seq):
    wqk = 2 * HEAD_DIM
    return pl.pallas_call(
        functools.partial(_mla_kernel, seq=seq),
        grid=(batch, MLA_HEADS),
        in_specs=[pl.BlockSpec((seq, wqk), lambda b, h: (b, h)),
                  pl.BlockSpec((seq, wqk), lambda b, h: (b, h)),
                  pl.BlockSpec((seq, MLA_V), lambda b, h: (b, h))],
        out_specs=pl.BlockSpec((seq, MLA_V), lambda b, h: (b, h)),
        out_shape=jax.ShapeDtypeStruct((batch * seq, WC), BF16),
        compiler_params=_params(("parallel", "parallel")),
        name="mla_attention",
    )(qm, km, vm)


def _merge_kernel(oa_ref, ob_ref, oc_ref, ga_ref, gb_ref, gc_ref, x_ref, wbr_ref, wout_ref, o_ref):
    ma = _dot(oa_ref[...], wbr_ref[0:WA, :])
    mb = _dot(ob_ref[...], wbr_ref[WA:WA + WB, :])
    mc = _dot(oc_ref[...], wbr_ref[WA + WB:, :])
    merged = (jax.nn.sigmoid(ga_ref[...].astype(F32)) * ma
              + jax.nn.sigmoid(gb_ref[...].astype(F32)) * mb
              + jax.nn.sigmoid(gc_ref[...].astype(F32)) * mc)
    o_ref[...] = x_ref[...] + _dot(merged.astype(BF16), wout_ref[...])


def _merge(o_a, o_b, o_c, y, x2, w_branch, w_out):
    t = x2.shape[0]
    tm = 256
    gcol = COL_GATES // D_MODEL
    return pl.pallas_call(
        _merge_kernel,
        grid=(t // tm,),
        in_specs=[pl.BlockSpec((tm, WA), lambda i: (i, 0)),
                  pl.BlockSpec((tm, WB), lambda i: (i, 0)),
                  pl.BlockSpec((tm, WC), lambda i: (i, 0)),
                  pl.BlockSpec((tm, D_MODEL), lambda i: (i, gcol)),
                  pl.BlockSpec((tm, D_MODEL), lambda i: (i, gcol + 1)),
                  pl.BlockSpec((tm, D_MODEL), lambda i: (i, gcol + 2)),
                  pl.BlockSpec((tm, D_MODEL), lambda i: (i, 0)),
                  _resident(w_branch.shape), _resident(w_out.shape)],
        out_specs=pl.BlockSpec((tm, D_MODEL), lambda i: (i, 0)),
        out_shape=jax.ShapeDtypeStruct((t, D_MODEL), F32),
        compiler_params=_params(("parallel",)),
        name="gated_merge",
    )(o_a, o_b, o_c, y, y, y, x2, w_branch, w_out)


def _mem_kv_kernel(mem_ref, g_ref, wkv_ref, kg_ref, k_ref, v_ref):
    memn = _rms(mem_ref[...], g_ref[...], D_MODEL).astype(BF16)
    kv = _dot(memn, wkv_ref[...])
    for h in range(XA_HEADS):
        cols = slice(h * HEAD_DIM, (h + 1) * HEAD_DIM)
        k_ref[:, cols] = _rms(kv[:, cols], kg_ref[...], HEAD_DIM).astype(BF16)
    v_ref[...] = kv[:, XA_W:].astype(BF16)


def _mem_kv(mem2, gain, w_kv, k_gain, batch, mem_len):
    return pl.pallas_call(
        _mem_kv_kernel,
        grid=(batch,),
        in_specs=[pl.BlockSpec((mem_len, D_MODEL), lambda b: (b, 0)),
                  _resident(gain.shape), _resident(w_kv.shape), _resident(k_gain.shape)],
        out_specs=[pl.BlockSpec((mem_len, XA_W), lambda b: (b, 0)),
                   pl.BlockSpec((mem_len, XA_W), lambda b: (b, 0))],
        out_shape=[jax.ShapeDtypeStruct((batch * mem_len, XA_W), BF16)] * 2,
        compiler_params=_params(("parallel",)),
        name="mem_kv",
    )(mem2, gain, w_kv, k_gain)


def _xattn_kernel(x_ref, g_ref, wq_ref, qg_ref, k_ref, v_ref, wo_ref, o_ref):
    scale = HEAD_DIM ** -0.5
    x = x_ref[...]
    xn = _rms(x, g_ref[...], D_MODEL).astype(BF16)
    q = _dot(xn, wq_ref[...])
    heads = []
    for h in range(XA_HEADS):
        cols = slice(h * HEAD_DIM, (h + 1) * HEAD_DIM)
        qh = (_rms(q[:, cols], qg_ref[...], HEAD_DIM) * scale).astype(BF16)
        s = _dot_t(qh, k_ref[:, cols])
        p = jnp.exp(s - jnp.max(s, axis=-1, keepdims=True))
        oh = _dot(p.astype(BF16), v_ref[:, cols]) / jnp.sum(p, axis=-1, keepdims=True)
        heads.append(oh.astype(BF16))
    o = jnp.concatenate(heads, axis=-1)
    o_ref[...] = x + _dot(o, wo_ref[...])


def _xattn(x2, gain, w_q, q_gain, k_mem, v_mem, w_o, seq, mem_len):
    t = x2.shape[0]
    tm = 512
    per_batch = seq // tm
    return pl.pallas_call(
        _xattn_kernel,
        grid=(t // tm,),
        in_specs=[pl.BlockSpec((tm, D_MODEL), lambda i: (i, 0)),
                  _resident(gain.shape), _resident(w_q.shape), _resident(q_gain.shape),
                  pl.BlockSpec((mem_len, XA_W), lambda i: (i // per_batch, 0)),
                  pl.BlockSpec((mem_len, XA_W), lambda i: (i // per_batch, 0)),
                  _resident(w_o.shape)],
        out_specs=pl.BlockSpec((tm, D_MODEL), lambda i: (i, 0)),
        out_shape=jax.ShapeDtypeStruct((t, D_MODEL), F32),
        compiler_params=_params(("parallel",)),
        name="mem_xattn",
    )(x2, gain, w_q, q_gain, k_mem, v_mem, w_o)


def _ffn_kernel(x_ref, g_ref, wg_ref, wu_ref, wd_ref, o_ref, h_ref):
    @pl.when(pl.program_id(1) == 0)
    def _():
        x = x_ref[...]
        h_ref[...] = _rms(x, g_ref[...], D_MODEL).astype(BF16)
        o_ref[...] = x

    h = h_ref[...]
    gate = _dot(h, wg_ref[...])
    up = _dot(h, wu_ref[...])
    act = (gate * jax.nn.sigmoid(gate) * up).astype(BF16)
    o_ref[...] += _dot(act, wd_ref[...])


def _ffn(x2, gain, w_gu, w_down):
    t = x2.shape[0]
    tm, tf = 512, 512
    nf = D_FF // tf
    return pl.pallas_call(
        _ffn_kernel,
        grid=(t // tm, nf),
        in_specs=[pl.BlockSpec((tm, D_MODEL), lambda i, f: (i, 0)),
                  pl.BlockSpec((1, D_MODEL), lambda i, f: (0, 0)),
                  pl.BlockSpec((D_MODEL, tf), lambda i, f: (0, f)),
                  pl.BlockSpec((D_MODEL, tf), lambda i, f: (0, nf + f)),
                  pl.BlockSpec((tf, D_MODEL), lambda i, f: (f, 0))],
        out_specs=pl.BlockSpec((tm, D_MODEL), lambda i, f: (i, 0)),
        out_shape=jax.ShapeDtypeStruct((t, D_MODEL), F32),
        scratch_shapes=[pltpu.VMEM((tm, D_MODEL), BF16)],
        compiler_params=_params(("parallel", "arbitrary")),
        name="swiglu_ffn",
    )(x2, gain, w_gu, w_gu, w_down)


def _in_proj_weight(w):
    kr0 = COL_GATES
    main, kr, gates = w[:, :kr0], w[:, kr0:kr0 + MLA_ROPE], w[:, kr0 + MLA_ROPE:]
    pad = jnp.zeros((w.shape[0], IN_W_PAD - COL_KR - MLA_ROPE), w.dtype)
    return jnp.concatenate([main, gates, kr, pad], axis=1).astype(BF16)


def _mla_uq_weight(w):
    r = w.shape[0]
    w = w.reshape(r, MLA_HEADS, MLA_QK)
    nope = w[:, :, :MLA_NOPE].reshape(r, MLA_HEADS * MLA_NOPE)
    rope = jnp.pad(w[:, :, MLA_NOPE:], ((0, 0), (0, 0), (0, HEAD_DIM - MLA_ROPE)))
    return jnp.concatenate([nope, rope.reshape(r, MLA_HEADS * HEAD_DIM)], axis=1).astype(BF16)


def _split_heads_weight(w, heads, first):
    k = w.shape[0]
    w = w.reshape(k, heads, -1)
    a = w[:, :, :first].reshape(k, -1)
    b = w[:, :, first:].reshape(k, -1)
    return jnp.concatenate([a, b], axis=1).astype(BF16)


def kernel(x, mem, positions, rel_bias, norm_mix, w_in, swa_q_norm, swa_k_norm, swa_sinks,
           mla_cq_norm, mla_ckv_norm, mla_w_uq, mla_w_ukv, mla_q_norm, mla_k_norm,
           w_branch, w_out, norm_xa, norm_mem, xa_wq, xa_wkv, xa_q_norm, xa_k_norm, xa_wo,
           norm_ffn, ffn_w_gu, ffn_w_down):
    batch, seq, d = x.shape
    mem_len = mem.shape[1]
    depth = w_in.shape[0]
    assert d == D_MODEL and seq % 512 == 0 and (batch * seq) % 1024 == 0 and mem_len % 8 == 0
    row = lambda v: v.reshape(1, -1)

    x2 = x.reshape(batch * seq, d)
    mem2 = mem.reshape(batch * mem_len, d)
    cos_t, sin_t = _rope_tables(positions)
    swa_bias = _swa_bias_table(rel_bias)

    for l in range(depth):
        y = _in_proj(x2, row(norm_mix[l]), _in_proj_weight(w_in[l]))
        o_a = _swa_attention(y, swa_bias, row(swa_q_norm[l]), row(swa_k_norm[l]),
                             row(swa_sinks[l]), batch, seq)
        o_b = _sb_attention(y, batch, seq)
        qm, km, vm = _mla_prep(y, cos_t, sin_t, _mla_uq_weight(mla_w_uq[l]),
                               _split_heads_weight(mla_w_ukv[l], MLA_HEADS, MLA_NOPE),
                               mla_cq_norm[l], mla_ckv_norm[l], mla_q_norm[l], mla_k_norm[l])
        o_c = _mla_attention(qm, km, vm, batch, seq)
        x2 = _merge(o_a, o_b, o_c, y, x2, w_branch[l].astype(BF16), w_out[l].astype(BF16))

        k_mem, v_mem = _mem_kv(mem2, row(norm_mem[l]),
                               _split_heads_weight(xa_wkv[l], XA_HEADS, HEAD_DIM),
                               row(xa_k_norm[l]), batch, mem_len)
        x2 = _xattn(x2, row(norm_xa[l]), xa_wq[l].astype(BF16), row(xa_q_norm[l]),
                    k_mem, v_mem, xa_wo[l].astype(BF16), seq, mem_len)
        x2 = _ffn(x2, row(norm_ffn[l]), ffn_w_gu[l].astype(BF16), ffn_w_down[l].astype(BF16))
    return x2.reshape(batch, seq, d)
```

```python
import functools
import math

import jax
import jax.numpy as jnp
from jax import lax
from jax.experimental import pallas as pl
from jax.experimental.pallas import tpu as pltpu

F32 = jnp.float32
BF16 = jnp.bfloat16

D_MODEL = 2048
HEAD_DIM = 128
BLOCK = 128
SWA_Q_HEADS = 8
SWA_KV_HEADS = 2
WINDOW = 128
SB_HEADS = 4
MLA_HEADS = 4
MLA_Q_RANK = 512
MLA_KV_RANK = 512
MLA_NOPE = 128
MLA_ROPE = 64
MLA_V = 128
ROPE_THETA = 10000.0
XA_HEADS = 4
REL_BUCKETS = 32
REL_MAX_DIST = 128
D_FF = ((8 * D_MODEL + 3 * 256 - 1) // (3 * 256)) * 256
EPS = 1e-6

WA = SWA_Q_HEADS * HEAD_DIM
KVA = SWA_KV_HEADS * HEAD_DIM
WB = SB_HEADS * HEAD_DIM
WC = MLA_HEADS * MLA_V
XA_W = XA_HEADS * HEAD_DIM
MLA_QK = MLA_NOPE + MLA_ROPE

COL_QA = 0
COL_KA = COL_QA + WA
COL_VA = COL_KA + KVA
COL_QB = COL_VA + KVA
COL_KB = COL_QB + WB
COL_VB = COL_KB + WB
COL_CQ = COL_VB + WB
COL_CKV = COL_CQ + MLA_Q_RANK
COL_GATES = COL_CKV + MLA_KV_RANK
COL_KR = COL_GATES + 3 * D_MODEL
IN_TILE_N = 1536
IN_W_PAD = -(-(COL_KR + HEAD_DIM) // IN_TILE_N) * IN_TILE_N

SWA_BLOCKS_PER_STEP = 4
LOG2_E = math.log2(math.e)
SB_SCORE_SCALE = HEAD_DIM ** -0.5 * LOG2_E
SB_SKEW = 1
ATT_TILE = 256
MASK_VALUE = -1e30
V7X_VMEM_LIMIT_BYTES = 56 * 1024 * 1024


def _params(semantics, flags=None):
    return pltpu.CompilerParams(dimension_semantics=semantics,
                                vmem_limit_bytes=V7X_VMEM_LIMIT_BYTES, flags=flags)


def _rms(x, gain, width):
    ms = jnp.sum(x * x, axis=-1, keepdims=True) * (1.0 / width)
    return x * lax.rsqrt(ms + EPS) * gain


def _dot(a, b):
    return jnp.dot(a, b, preferred_element_type=F32)


def _dot_t(a, b):
    return lax.dot_general(a, b, (((1,), (1,)), ((), ())), preferred_element_type=F32)


def _resident(shape):
    nd = len(shape)
    return pl.BlockSpec(shape, lambda *_: (0,) * nd, pipeline_mode=pl.Buffered(1))


def _rope_table_kernel(pos_ref, inv_ref, cos_ref, sin_ref):
    ang = pos_ref[...].astype(F32) * inv_ref[...]
    lane = lax.broadcasted_iota(jnp.int32, ang.shape, 1)
    cos_ref[...] = jnp.cos(ang)
    s = jnp.sin(ang)
    sin_ref[...] = jnp.where(lane < HEAD_DIM // 2, -s, s)


def _rope_tables(positions):
    t = positions.size
    half = MLA_ROPE // 2
    inv = ROPE_THETA ** (-jnp.arange(half, dtype=F32) / half)
    inv_l = jnp.tile(inv, HEAD_DIM // half).reshape(1, HEAD_DIM)
    pos_l = jnp.broadcast_to(positions.reshape(t, 1), (t, HEAD_DIM))
    tm = 1024
    return pl.pallas_call(
        _rope_table_kernel,
        grid=(t // tm,),
        in_specs=[pl.BlockSpec((tm, HEAD_DIM), lambda i: (i, 0)),
                  pl.BlockSpec((1, HEAD_DIM), lambda i: (0, 0))],
        out_specs=[pl.BlockSpec((tm, HEAD_DIM), lambda i: (i, 0)),
                   pl.BlockSpec((tm, HEAD_DIM), lambda i: (i, 0))],
        out_shape=[jax.ShapeDtypeStruct((t, HEAD_DIM), F32)] * 2,
        compiler_params=_params(("parallel",)),
        name="rope_tables",
    )(pos_l, inv_l)


def _t5_bucket(rel):
    n = jnp.maximum(rel, 0)
    exact = REL_BUCKETS // 2
    nf = jnp.maximum(n, exact).astype(F32)
    large = exact + (jnp.log(nf / exact) / math.log(REL_MAX_DIST / exact)
                     * (REL_BUCKETS - exact)).astype(jnp.int32)
    large = jnp.minimum(large, REL_BUCKETS - 1)
    return jnp.where(n < exact, n, large)


def _swa_bias_kernel(bucket_ref, rel_bias_ref, out_ref):
    h = pl.program_id(0)
    bucket = bucket_ref[...]
    bias = jnp.zeros(bucket.shape, F32)
    for b in range(REL_BUCKETS):
        bias = jnp.where(bucket == b, rel_bias_ref[b, h] * LOG2_E, bias)
    i = lax.broadcasted_iota(jnp.int32, bucket.shape, 0)
    j = lax.broadcasted_iota(jnp.int32, bucket.shape, 1)
    rel = BLOCK + i - j
    in_window = (rel >= 0) & (rel < WINDOW)
    out_ref[0, 0] = jnp.where(in_window & (j >= BLOCK), bias, MASK_VALUE)
    out_ref[1, 0] = jnp.where(in_window, bias, MASK_VALUE)


def _swa_bias_table(rel_bias):
    i = jnp.arange(BLOCK)[:, None]
    j = jnp.arange(2 * BLOCK)[None, :]
    bucket = _t5_bucket(BLOCK + i - j).astype(jnp.int32)
    return pl.pallas_call(
        _swa_bias_kernel,
        grid=(SWA_Q_HEADS,),
        in_specs=[pl.BlockSpec((BLOCK, 2 * BLOCK), lambda h: (0, 0)),
                  pl.BlockSpec(memory_space=pltpu.SMEM)],
        out_specs=pl.BlockSpec((2, 1, BLOCK, 2 * BLOCK), lambda h: (0, h, 0, 0)),
        out_shape=jax.ShapeDtypeStruct((2, SWA_Q_HEADS, BLOCK, 2 * BLOCK), F32),
        compiler_params=_params(("parallel",)),
        name="swa_bias_table",
    )(bucket, rel_bias)


def _in_proj_kernel(x_ref, g_ref, w_ref, y_ref, h_ref):
    @pl.when(pl.program_id(1) == 0)
    def _():
        h_ref[...] = _rms(x_ref[...], g_ref[...], D_MODEL).astype(BF16)

    y_ref[...] = _dot(h_ref[...], w_ref[...]).astype(BF16)


def _in_proj(x2, gain, w):
    t = x2.shape[0]
    tm, tn = 1024, IN_TILE_N
    return pl.pallas_call(
        _in_proj_kernel,
        grid=(t // tm, IN_W_PAD // tn),
        in_specs=[pl.BlockSpec((tm, D_MODEL), lambda i, j: (i, 0)),
                  pl.BlockSpec((1, D_MODEL), lambda i, j: (0, 0)),
                  pl.BlockSpec((D_MODEL, tn), lambda i, j: (0, j))],
        out_specs=pl.BlockSpec((tm, tn), lambda i, j: (i, j)),
        out_shape=jax.ShapeDtypeStruct((t, IN_W_PAD), BF16),
        scratch_shapes=[pltpu.VMEM((tm, D_MODEL), BF16)],
        compiler_params=_params(("parallel", "arbitrary")),
        name="in_proj",
    )(x2, gain, w)


def _swa_kernel(q_ref, kp_ref, kc_ref, vp_ref, vc_ref, bias_ref, qg_ref, kg_ref, sink_ref, o_ref):
    group = SWA_Q_HEADS // SWA_KV_HEADS
    first_variant = jnp.minimum(pl.program_id(1), 1)
    for hk in range(SWA_KV_HEADS):
        cols = slice(hk * HEAD_DIM, (hk + 1) * HEAD_DIM)
        k_all = jnp.concatenate([kp_ref[:, cols], kc_ref[:, cols]], axis=0).astype(F32)
        k_all = _rms(k_all, kg_ref[...], HEAD_DIM).astype(BF16)
        v_all = jnp.concatenate([vp_ref[:, cols], vc_ref[:, cols]], axis=0)
        for blk in range(SWA_BLOCKS_PER_STEP):
            rows = slice(blk * BLOCK, (blk + 1) * BLOCK)
            band = slice(blk * BLOCK, (blk + 2) * BLOCK)
            k, v = k_all[band], v_all[band]
            for g in range(group):
                h = hk * group + g
                hcols = slice(h * HEAD_DIM, (h + 1) * HEAD_DIM)
                q = _rms(q_ref[rows, hcols].astype(F32), qg_ref[...], HEAD_DIM).astype(BF16)
                bias = bias_ref[first_variant, h] if blk == 0 else bias_ref[1, h]
                s = _dot_t(q, k) + bias
                sink = sink_ref[0, h] * LOG2_E
                m = jnp.maximum(jnp.max(s, axis=-1, keepdims=True), sink)
                p = jnp.exp2(s - m)
                denom = jnp.sum(p, axis=-1, keepdims=True) + jnp.exp2(sink - m)
                o = _dot(p.astype(BF16), v) / denom
                o_ref[rows, hcols] = o.astype(BF16)


def _swa_attention(y, bias, q_gain, k_gain, sinks, batch, seq):
    span = SWA_BLOCKS_PER_STEP * BLOCK
    steps = seq // span
    kcol = COL_KA // KVA
    vcol = COL_VA // KVA
    row = lambda b, n: b * steps + n
    prev = lambda b, n: (b * steps + n) * SWA_BLOCKS_PER_STEP - jnp.minimum(n, 1)
    return pl.pallas_call(
        _swa_kernel,
        grid=(batch, steps),
        in_specs=[pl.BlockSpec((span, WA), lambda b, n: (row(b, n), COL_QA // WA)),
                  pl.BlockSpec((BLOCK, KVA), lambda b, n: (prev(b, n), kcol)),
                  pl.BlockSpec((span, KVA), lambda b, n: (row(b, n), kcol)),
                  pl.BlockSpec((BLOCK, KVA), lambda b, n: (prev(b, n), vcol)),
                  pl.BlockSpec((span, KVA), lambda b, n: (row(b, n), vcol)),
                  _resident(bias.shape),
                  pl.BlockSpec((1, HEAD_DIM), lambda b, n: (0, 0)),
                  pl.BlockSpec((1, HEAD_DIM), lambda b, n: (0, 0)),
                  pl.BlockSpec(memory_space=pltpu.SMEM)],
        out_specs=pl.BlockSpec((span, WA), lambda b, n: (row(b, n), 0)),
        out_shape=jax.ShapeDtypeStruct((batch * seq, WA), BF16),
        compiler_params=_params(("parallel", "arbitrary")),
        name="swa_attention",
    )(y, y, y, y, y, bias, q_gain, k_gain, sinks)


def _sb_kernel(q_ref, k_ref, v_ref, o_ref, *, seq):
    r = lax.broadcasted_iota(jnp.int32, (ATT_TILE, ATT_TILE), 0)
    c = lax.broadcasted_iota(jnp.int32, (ATT_TILE, ATT_TILE), 1)
    later = jnp.where(r > c, 1.0, 0.0).astype(BF16)
    earlier = c < r
    sign_bit = jnp.uint32(0x80000000)
    blocks = [(i, j) for i in range(seq // ATT_TILE) for j in range(i, -1, -1)]
    z2s, own, betweens, runs, accs = {}, {}, {}, {}, {}

    def scores(n):
        i, j = blocks[n]
        q = q_ref[i * ATT_TILE:(i + 1) * ATT_TILE, :]
        z2s[n] = _dot_t(q, k_ref[j * ATT_TILE:(j + 1) * ATT_TILE, :])

    def drops(n):
        i, j = blocks[n]
        z2 = z2s.pop(n)
        neg_abs = pltpu.bitcast(pltpu.bitcast(z2, jnp.uint32) | sign_bit, F32)
        drop2 = jnp.maximum(z2, 0.0) + jnp.log2(1.0 + jnp.exp2(neg_abs))
        if j == i:
            drop2 = jnp.where(earlier, drop2, 0.0)
            runs[n] = jnp.zeros((ATT_TILE, 1), F32)
        if j > 0:
            runs[n + 1] = runs[n] + jnp.sum(drop2, axis=-1, keepdims=True)
        betweens[n] = _dot(drop2.astype(BF16), later)
        own[n] = z2 - drop2

    def weigh(n):
        i, j = blocks[n]
        w = jnp.exp2(own.pop(n) - betweens.pop(n) - runs.pop(n))
        if j == i:
            w = jnp.where(earlier, w, 0.0)
        pv = _dot(w.astype(BF16), v_ref[j * ATT_TILE:(j + 1) * ATT_TILE, :])
        accs[i] = pv if j == i else accs[i] + pv
        if j == 0:
            o_ref[i * ATT_TILE:(i + 1) * ATT_TILE, :] = accs.pop(i).astype(BF16)

    for t in range(len(blocks) + 2 * SB_SKEW):
        if t < len(blocks):
            scores(t)
        if 0 <= t - SB_SKEW < len(blocks):
            drops(t - SB_SKEW)
        if 0 <= t - 2 * SB_SKEW < len(blocks):
            weigh(t - 2 * SB_SKEW)


def _sb_attention(y, batch, seq):
    qc, kc, vc = COL_QB // HEAD_DIM, COL_KB // HEAD_DIM, COL_VB // HEAD_DIM
    return pl.pallas_call(
        functools.partial(_sb_kernel, seq=seq),
        grid=(batch, SB_HEADS),
        in_specs=[pl.BlockSpec((seq, HEAD_DIM), lambda b, h: (b, qc + h)),
                  pl.BlockSpec((seq, HEAD_DIM), lambda b, h: (b, kc + h)),
                  pl.BlockSpec((seq, HEAD_DIM), lambda b, h: (b, vc + h))],
        out_specs=pl.BlockSpec((seq, HEAD_DIM), lambda b, h: (b, h)),
        out_shape=jax.ShapeDtypeStruct((batch * seq, WB), BF16),
        compiler_params=_params(("parallel", "parallel")),
        name="sb_attention",
    )(y, y, y)


def _spread_rope(v):
    half = MLA_ROPE // 2
    zeros = jnp.zeros(v.shape[:-1] + (HEAD_DIM // 2 - half,), v.dtype)
    return jnp.concatenate([v[..., :half], zeros, v[..., half:], zeros], axis=-1)


def _rope(x, cos, sin_signed):
    return x * cos + pltpu.roll(x, HEAD_DIM // 2, 1) * sin_signed


def _mla_prep_kernel(cq_ref, ckv_ref, kr_ref, cos_ref, sin_ref, wuq_ref, wukv_ref,
                     cqg_ref, ckvg_ref, qgn_ref, qgr_ref, kgn_ref, kgr_ref,
                     q_ref, k_ref, v_ref):
    scale = MLA_QK ** -0.5 * math.log2(math.e)
    rope0 = MLA_HEADS * MLA_NOPE
    cos, sin = cos_ref[...], sin_ref[...]
    cqn = _rms(cq_ref[...].astype(F32), cqg_ref[...], MLA_Q_RANK).astype(BF16)
    ckvn = _rms(ckv_ref[...].astype(F32), ckvg_ref[...], MLA_KV_RANK).astype(BF16)
    qc = _dot(cqn, wuq_ref[...])
    kvc = _dot(ckvn, wukv_ref[...])
    k_rope = _rope(_rms(kr_ref[...].astype(F32), kgr_ref[...], MLA_ROPE), cos, sin).astype(BF16)
    for h in range(MLA_HEADS):
        nope = slice(h * MLA_NOPE, (h + 1) * MLA_NOPE)
        spread_rope = slice(rope0 + h * HEAD_DIM, rope0 + (h + 1) * HEAD_DIM)
        qn = _rms(qc[:, nope], qgn_ref[...] * scale, MLA_NOPE)
        qr = _rope(_rms(qc[:, spread_rope], qgr_ref[...] * scale, MLA_ROPE), cos, sin)
        q_ref[:, 2 * h * HEAD_DIM:(2 * h + 1) * HEAD_DIM] = qn.astype(BF16)
        q_ref[:, (2 * h + 1) * HEAD_DIM:(2 * h + 2) * HEAD_DIM] = qr.astype(BF16)
        kn = _rms(kvc[:, nope], kgn_ref[...], MLA_NOPE)
        k_ref[:, 2 * h * HEAD_DIM:(2 * h + 1) * HEAD_DIM] = kn.astype(BF16)
        k_ref[:, (2 * h + 1) * HEAD_DIM:(2 * h + 2) * HEAD_DIM] = k_rope
    v_ref[...] = kvc[:, rope0:].astype(BF16)


def _mla_prep(y, cos_t, sin_t, w_uq, w_ukv, cq_gain, ckv_gain, q_gain, k_gain):
    t = y.shape[0]
    tm = 512
    row = lambda v: v.reshape(1, -1)
    gains = [row(cq_gain), row(ckv_gain),
             row(q_gain[:MLA_NOPE]), row(_spread_rope(q_gain[MLA_NOPE:])),
             row(k_gain[:MLA_NOPE]), row(_spread_rope(k_gain[MLA_NOPE:]))]
    wq = 2 * MLA_HEADS * HEAD_DIM
    return pl.pallas_call(
        _mla_prep_kernel,
        grid=(t // tm,),
        in_specs=[pl.BlockSpec((tm, MLA_Q_RANK), lambda i: (i, COL_CQ // MLA_Q_RANK)),
                  pl.BlockSpec((tm, MLA_KV_RANK), lambda i: (i, COL_CKV // MLA_KV_RANK)),
                  pl.BlockSpec((tm, HEAD_DIM), lambda i: (i, COL_KR // HEAD_DIM)),
                  pl.BlockSpec((tm, HEAD_DIM), lambda i: (i, 0)),
                  pl.BlockSpec((tm, HEAD_DIM), lambda i: (i, 0)),
                  _resident(w_uq.shape), _resident(w_ukv.shape)]
                 + [_resident(g.shape) for g in gains],
        out_specs=[pl.BlockSpec((tm, wq), lambda i: (i, 0)),
                   pl.BlockSpec((tm, wq), lambda i: (i, 0)),
                   pl.BlockSpec((tm, WC), lambda i: (i, 0))],
        out_shape=[jax.ShapeDtypeStruct((t, wq), BF16),
                   jax.ShapeDtypeStruct((t, wq), BF16),
                   jax.ShapeDtypeStruct((t, WC), BF16)],
        compiler_params=_params(("parallel",)),
        name="mla_prep",
    )(y, y, y, cos_t, sin_t, w_uq, w_ukv, *gains)


def _mla_kernel(q_ref, k_ref, v_ref, o_ref, *, seq):
    r = lax.broadcasted_iota(jnp.int32, (ATT_TILE, ATT_TILE), 0)
    c = lax.broadcasted_iota(jnp.int32, (ATT_TILE, ATT_TILE), 1)
    causal = c <= r
    tiles = seq // ATT_TILE
    scores = {}

    def score(i):
        q = q_ref[i * ATT_TILE:(i + 1) * ATT_TILE, :]
        s = [_dot_t(q, k_ref[j * ATT_TILE:(j + 1) * ATT_TILE, :]) for j in range(i + 1)]
        s[i] = jnp.where(causal, s[i], MASK_VALUE)
        scores[i] = s

    def attend(i):
        s = scores.pop(i)
        m = functools.reduce(jnp.maximum, [jnp.max(sj, axis=-1, keepdims=True) for sj in s])
        p = [jnp.exp2(sj - m) for sj in s]
        l = functools.reduce(jnp.add, [jnp.sum(pj, axis=-1, keepdims=True) for pj in p])
        p_all = jnp.concatenate([pj.astype(BF16) for pj in p], axis=1)
        acc = _dot(p_all, v_ref[0:(i + 1) * ATT_TILE, :])
        o_ref[i * ATT_TILE:(i + 1) * ATT_TILE, :] = (acc / l).astype(BF16)

    for t in range(tiles + 1):
        if t < tiles:
            score(t)
        if t >= 1:
            attend(t - 1)


def _mla_attention(qm, km, vm, batch, seq):
    wqk = 2 * HEAD_DIM
    return pl.pallas_call(
        functools.partial(_mla_kernel, seq=seq),
        grid=(batch, MLA_HEADS),
        in_specs=[pl.BlockSpec((seq, wqk), lambda b, h: (b, h)),
                  pl.BlockSpec((seq, wqk), lambda b, h: (b, h)),
                  pl.BlockSpec((seq, MLA_V), lambda b, h: (b, h))],
        out_specs=pl.BlockSpec((seq, MLA_V), lambda b, h: (b, h)),
        out_shape=jax.ShapeDtypeStruct((batch * seq, WC), BF16),
        compiler_params=_params(("parallel", "parallel")),
        name="mla_attention",
    )(qm, km, vm)


def _merge_kernel(oa_ref, ob_ref, oc_ref, ga_ref, gb_ref, gc_ref, x_ref, wbr_ref, wout_ref, o_ref):
    ma = _dot(oa_ref[...], wbr_ref[0:WA, :])
    mb = _dot(ob_ref[...], wbr_ref[WA:WA + WB, :])
    mc = _dot(oc_ref[...], wbr_ref[WA + WB:, :])
    merged = (jax.nn.sigmoid(ga_ref[...].astype(F32)) * ma
              + jax.nn.sigmoid(gb_ref[...].astype(F32)) * mb
              + jax.nn.sigmoid(gc_ref[...].astype(F32)) * mc)
    o_ref[...] = x_ref[...] + _dot(merged.astype(BF16), wout_ref[...])


def _merge(o_a, o_b, o_c, y, x2, w_branch, w_out):
    t = x2.shape[0]
    tm = 512
    gcol = COL_GATES // D_MODEL
    return pl.pallas_call(
        _merge_kernel,
        grid=(t // tm,),
        in_specs=[pl.BlockSpec((tm, WA), lambda i: (i, 0)),
                  pl.BlockSpec((tm, WB), lambda i: (i, 0)),
                  pl.BlockSpec((tm, WC), lambda i: (i, 0)),
                  pl.BlockSpec((tm, D_MODEL), lambda i: (i, gcol)),
                  pl.BlockSpec((tm, D_MODEL), lambda i: (i, gcol + 1)),
                  pl.BlockSpec((tm, D_MODEL), lambda i: (i, gcol + 2)),
                  pl.BlockSpec((tm, D_MODEL), lambda i: (i, 0)),
                  _resident(w_branch.shape), _resident(w_out.shape)],
        out_specs=pl.BlockSpec((tm, D_MODEL), lambda i: (i, 0)),
        out_shape=jax.ShapeDtypeStruct((t, D_MODEL), F32),
        compiler_params=_params(("parallel",)),
        name="gated_merge",
    )(o_a, o_b, o_c, y, y, y, x2, w_branch, w_out)


def _mem_kv_kernel(mem_ref, g_ref, wkv_ref, kg_ref, k_ref, v_ref):
    memn = _rms(mem_ref[...], g_ref[...], D_MODEL).astype(BF16)
    kv = _dot(memn, wkv_ref[...])
    for h in range(XA_HEADS):
        cols = slice(h * HEAD_DIM, (h + 1) * HEAD_DIM)
        k_ref[:, cols] = _rms(kv[:, cols], kg_ref[...], HEAD_DIM).astype(BF16)
    v_ref[...] = kv[:, XA_W:].astype(BF16)


def _mem_kv(mem2, gain, w_kv, k_gain, batch, mem_len):
    return pl.pallas_call(
        _mem_kv_kernel,
        grid=(batch,),
        in_specs=[pl.BlockSpec((mem_len, D_MODEL), lambda b: (b, 0)),
                  _resident(gain.shape), _resident(w_kv.shape), _resident(k_gain.shape)],
        out_specs=[pl.BlockSpec((mem_len, XA_W), lambda b: (b, 0)),
                   pl.BlockSpec((mem_len, XA_W), lambda b: (b, 0))],
        out_shape=[jax.ShapeDtypeStruct((batch * mem_len, XA_W), BF16)] * 2,
        compiler_params=_params(("parallel",)),
        name="mem_kv",
    )(mem2, gain, w_kv, k_gain)


def _xattn_kernel(x_ref, g_ref, wq_ref, qg_ref, k_ref, v_ref, wo_ref, o_ref):
    scale = HEAD_DIM ** -0.5
    x = x_ref[...]
    xn = _rms(x, g_ref[...], D_MODEL).astype(BF16)
    q = _dot(xn, wq_ref[...])
    heads = []
    for h in range(XA_HEADS):
        cols = slice(h * HEAD_DIM, (h + 1) * HEAD_DIM)
        qh = (_rms(q[:, cols], qg_ref[...], HEAD_DIM) * scale).astype(BF16)
        s = _dot_t(qh, k_ref[:, cols])
        p = jnp.exp(s - jnp.max(s, axis=-1, keepdims=True))
        oh = _dot(p.astype(BF16), v_ref[:, cols]) / jnp.sum(p, axis=-1, keepdims=True)
        heads.append(oh.astype(BF16))
    o = jnp.concatenate(heads, axis=-1)
    o_ref[...] = x + _dot(o, wo_ref[...])


def _xattn(x2, gain, w_q, q_gain, k_mem, v_mem, w_o, seq, mem_len):
    t = x2.shape[0]
    tm = 512
    per_batch = seq // tm
    return pl.pallas_call(
        _xattn_kernel,
        grid=(t // tm,),
        in_specs=[pl.BlockSpec((tm, D_MODEL), lambda i: (i, 0)),
                  _resident(gain.shape), _resident(w_q.shape), _resident(q_gain.shape),
                  pl.BlockSpec((mem_len, XA_W), lambda i: (i // per_batch, 0)),
                  pl.BlockSpec((mem_len, XA_W), lambda i: (i // per_batch, 0)),
                  _resident(w_o.shape)],
        out_specs=pl.BlockSpec((tm, D_MODEL), lambda i: (i, 0)),
        out_shape=jax.ShapeDtypeStruct((t, D_MODEL), F32),
        compiler_params=_params(("parallel",)),
        name="mem_xattn",
    )(x2, gain, w_q, q_gain, k_mem, v_mem, w_o)


def _ffn_kernel(x_ref, g_ref, wg_ref, wu_ref, wd_ref, o_ref, h_ref):
    @pl.when(pl.program_id(1) == 0)
    def _():
        x = x_ref[...]
        h_ref[...] = _rms(x, g_ref[...], D_MODEL).astype(BF16)
        o_ref[...] = x

    h = h_ref[...]
    gate = _dot(h, wg_ref[...])
    up = _dot(h, wu_ref[...])
    act = (gate * jax.nn.sigmoid(gate) * up).astype(BF16)
    o_ref[...] += _dot(act, wd_ref[...])


def _ffn(x2, gain, w_gu, w_down):
    t = x2.shape[0]
    tm, tf = 1024, 512
    nf = D_FF // tf
    return pl.pallas_call(
        _ffn_kernel,
        grid=(t // tm, nf),
        in_specs=[pl.BlockSpec((tm, D_MODEL), lambda i, f: (i, 0)),
                  pl.BlockSpec((1, D_MODEL), lambda i, f: (0, 0)),
                  pl.BlockSpec((D_MODEL, tf), lambda i, f: (0, f)),
                  pl.BlockSpec((D_MODEL, tf), lambda i, f: (0, nf + f)),
                  pl.BlockSpec((tf, D_MODEL), lambda i, f: (f, 0))],
        out_specs=pl.BlockSpec((tm, D_MODEL), lambda i, f: (i, 0)),
        out_shape=jax.ShapeDtypeStruct((t, D_MODEL), F32),
        scratch_shapes=[pltpu.VMEM((tm, D_MODEL), BF16)],
        compiler_params=_params(("parallel", "arbitrary")),
        name="swiglu_ffn",
    )(x2, gain, w_gu, w_gu, w_down)


def _in_proj_weight(w):
    kr0 = COL_GATES
    col = jnp.arange(kr0)
    sb_q = (col >= COL_QB) & (col < COL_QB + WB)
    main = w[:, :kr0] * jnp.where(sb_q, SB_SCORE_SCALE, 1.0).astype(w.dtype)
    kr, gates = _spread_rope(w[:, kr0:kr0 + MLA_ROPE]), w[:, kr0 + MLA_ROPE:]
    pad = jnp.zeros((w.shape[0], IN_W_PAD - COL_KR - HEAD_DIM), w.dtype)
    return jnp.concatenate([main, gates, kr, pad], axis=1).astype(BF16)


def _mla_uq_weight(w):
    r = w.shape[0]
    w = w.reshape(r, MLA_HEADS, MLA_QK)
    nope = w[:, :, :MLA_NOPE].reshape(r, MLA_HEADS * MLA_NOPE)
    rope = _spread_rope(w[:, :, MLA_NOPE:])
    return jnp.concatenate([nope, rope.reshape(r, MLA_HEADS * HEAD_DIM)], axis=1).astype(BF16)


def _split_heads_weight(w, heads, first):
    k = w.shape[0]
    w = w.reshape(k, heads, -1)
    a = w[:, :, :first].reshape(k, -1)
    b = w[:, :, first:].reshape(k, -1)
    return jnp.concatenate([a, b], axis=1).astype(BF16)


def kernel(x, mem, positions, rel_bias, norm_mix, w_in, swa_q_norm, swa_k_norm, swa_sinks,
           mla_cq_norm, mla_ckv_norm, mla_w_uq, mla_w_ukv, mla_q_norm, mla_k_norm,
           w_branch, w_out, norm_xa, norm_mem, xa_wq, xa_wkv, xa_q_norm, xa_k_norm, xa_wo,
           norm_ffn, ffn_w_gu, ffn_w_down):
    batch, seq, d = x.shape
    mem_len = mem.shape[1]
    depth = w_in.shape[0]
    assert d == D_MODEL and seq % 512 == 0 and (batch * seq) % 1024 == 0 and mem_len % 8 == 0
    row = lambda v: v.reshape(1, -1)

    x2 = x.reshape(batch * seq, d)
    mem2 = mem.reshape(batch * mem_len, d)
    cos_t, sin_t = _rope_tables(positions)
    swa_bias = _swa_bias_table(rel_bias)

    for l in range(depth):
        y = _in_proj(x2, row(norm_mix[l]), _in_proj_weight(w_in[l]))
        o_a = _swa_attention(y, swa_bias, row(swa_q_norm[l] * (HEAD_DIM ** -0.5 * LOG2_E)),
                             row(swa_k_norm[l]),
                             row(swa_sinks[l]), batch, seq)
        o_b = _sb_attention(y, batch, seq)
        qm, km, vm = _mla_prep(y, cos_t, sin_t, _mla_uq_weight(mla_w_uq[l]),
                               _split_heads_weight(mla_w_ukv[l], MLA_HEADS, MLA_NOPE),
                               mla_cq_norm[l], mla_ckv_norm[l], mla_q_norm[l], mla_k_norm[l])
        o_c = _mla_attention(qm, km, vm, batch, seq)
        x2 = _merge(o_a, o_b, o_c, y, x2, w_branch[l].astype(BF16), w_out[l].astype(BF16))

        k_mem, v_mem = _mem_kv(mem2, row(norm_mem[l]),
                               _split_heads_weight(xa_wkv[l], XA_HEADS, HEAD_DIM),
                               row(xa_k_norm[l]), batch, mem_len)
        x2 = _xattn(x2, row(norm_xa[l]), xa_wq[l].astype(BF16), row(xa_q_norm[l]),
                    k_mem, v_mem, xa_wo[l].astype(BF16), seq, mem_len)
        x2 = _ffn(x2, row(norm_ffn[l]), ffn_w_gu[l].astype(BF16), ffn_w_down[l].astype(BF16))
    return x2.reshape(batch, seq, d)
```

```python
import functools
import math

import jax
import jax.numpy as jnp
from jax import lax
from jax.experimental import pallas as pl
from jax.experimental.pallas import tpu as pltpu

F32 = jnp.float32
BF16 = jnp.bfloat16

D_MODEL = 2048
HEAD_DIM = 128
BLOCK = 128
SWA_Q_HEADS = 8
SWA_KV_HEADS = 2
WINDOW = 128
SB_HEADS = 4
MLA_HEADS = 4
MLA_Q_RANK = 512
MLA_KV_RANK = 512
MLA_NOPE = 128
MLA_ROPE = 64
MLA_V = 128
ROPE_THETA = 10000.0
XA_HEADS = 4
REL_BUCKETS = 32
REL_MAX_DIST = 128
D_FF = ((8 * D_MODEL + 3 * 256 - 1) // (3 * 256)) * 256
EPS = 1e-6

WA = SWA_Q_HEADS * HEAD_DIM
KVA = SWA_KV_HEADS * HEAD_DIM
WB = SB_HEADS * HEAD_DIM
WC = MLA_HEADS * MLA_V
XA_W = XA_HEADS * HEAD_DIM
MLA_QK = MLA_NOPE + MLA_ROPE

COL_QA = 0
COL_KA = COL_QA + WA
COL_VA = COL_KA + KVA
COL_QB = COL_VA + KVA
COL_KB = COL_QB + WB
COL_VB = COL_KB + WB
COL_CQ = COL_VB + WB
COL_CKV = COL_CQ + MLA_Q_RANK
COL_GATES = COL_CKV + MLA_KV_RANK
COL_KR = COL_GATES + 3 * D_MODEL
IN_TILE_N = 1536
IN_W_PAD = -(-(COL_KR + HEAD_DIM) // IN_TILE_N) * IN_TILE_N

SWA_BLOCKS_PER_STEP = 4
LOG2_E = math.log2(math.e)
SB_SCORE_SCALE = HEAD_DIM ** -0.5 * LOG2_E
SB_SKEW = 1
XA_SUBTILES = 2
ATT_TILE = 256
MASK_VALUE = -1e30
V7X_VMEM_LIMIT_BYTES = 56 * 1024 * 1024


def _params(semantics, flags=None):
    return pltpu.CompilerParams(dimension_semantics=semantics,
                                vmem_limit_bytes=V7X_VMEM_LIMIT_BYTES, flags=flags)


def _rms(x, gain, width):
    ms = jnp.sum(x * x, axis=-1, keepdims=True) * (1.0 / width)
    return x * lax.rsqrt(ms + EPS) * gain


def _dot(a, b):
    return jnp.dot(a, b, preferred_element_type=F32)


def _dot_t(a, b):
    return lax.dot_general(a, b, (((1,), (1,)), ((), ())), preferred_element_type=F32)


def _with_ones(v):
    return jnp.concatenate([v, jnp.ones(v.shape, v.dtype)], axis=1)


def _resident(shape):
    nd = len(shape)
    return pl.BlockSpec(shape, lambda *_: (0,) * nd, pipeline_mode=pl.Buffered(1))


def _resident_layer(stacked_shape, layer):
    nd = len(stacked_shape) - 1
    return pl.BlockSpec((None,) + tuple(stacked_shape[1:]), lambda *_: (layer,) + (0,) * nd,
                        pipeline_mode=pl.Buffered(1))


def _rope_table_kernel(pos_ref, inv_ref, cos_ref, sin_ref):
    ang = pos_ref[...].astype(F32) * inv_ref[...]
    lane = lax.broadcasted_iota(jnp.int32, ang.shape, 1)
    cos_ref[...] = jnp.cos(ang)
    s = jnp.sin(ang)
    sin_ref[...] = jnp.where(lane < HEAD_DIM // 2, -s, s)


def _rope_tables(positions):
    t = positions.size
    half = MLA_ROPE // 2
    inv = ROPE_THETA ** (-jnp.arange(half, dtype=F32) / half)
    inv_l = jnp.tile(inv, HEAD_DIM // half).reshape(1, HEAD_DIM)
    pos_l = jnp.broadcast_to(positions.reshape(t, 1), (t, HEAD_DIM))
    tm = 1024
    return pl.pallas_call(
        _rope_table_kernel,
        grid=(t // tm,),
        in_specs=[pl.BlockSpec((tm, HEAD_DIM), lambda i: (i, 0)),
                  pl.BlockSpec((1, HEAD_DIM), lambda i: (0, 0))],
        out_specs=[pl.BlockSpec((tm, HEAD_DIM), lambda i: (i, 0)),
                   pl.BlockSpec((tm, HEAD_DIM), lambda i: (i, 0))],
        out_shape=[jax.ShapeDtypeStruct((t, HEAD_DIM), F32)] * 2,
        compiler_params=_params(("parallel",)),
        name="rope_tables",
    )(pos_l, inv_l)


def _t5_bucket(rel):
    n = jnp.maximum(rel, 0)
    exact = REL_BUCKETS // 2
    nf = jnp.maximum(n, exact).astype(F32)
    large = exact + (jnp.log(nf / exact) / math.log(REL_MAX_DIST / exact)
                     * (REL_BUCKETS - exact)).astype(jnp.int32)
    large = jnp.minimum(large, REL_BUCKETS - 1)
    return jnp.where(n < exact, n, large)


def _swa_bias_kernel(bucket_ref, rel_bias_ref, out_ref):
    h = pl.program_id(0)
    bucket = bucket_ref[...]
    bias = jnp.zeros(bucket.shape, F32)
    for b in range(REL_BUCKETS):
        bias = jnp.where(bucket == b, rel_bias_ref[b, h] * LOG2_E, bias)
    i = lax.broadcasted_iota(jnp.int32, bucket.shape, 0)
    j = lax.broadcasted_iota(jnp.int32, bucket.shape, 1)
    rel = BLOCK + i - j
    in_window = (rel >= 0) & (rel < WINDOW)
    out_ref[0, 0] = jnp.where(in_window & (j >= BLOCK), bias, MASK_VALUE)
    out_ref[1, 0] = jnp.where(in_window, bias, MASK_VALUE)


def _swa_bias_table(rel_bias):
    i = jnp.arange(BLOCK)[:, None]
    j = jnp.arange(2 * BLOCK)[None, :]
    bucket = _t5_bucket(BLOCK + i - j).astype(jnp.int32)
    return pl.pallas_call(
        _swa_bias_kernel,
        grid=(SWA_Q_HEADS,),
        in_specs=[pl.BlockSpec((BLOCK, 2 * BLOCK), lambda h: (0, 0)),
                  pl.BlockSpec(memory_space=pltpu.SMEM)],
        out_specs=pl.BlockSpec((2, 1, BLOCK, 2 * BLOCK), lambda h: (0, h, 0, 0)),
        out_shape=jax.ShapeDtypeStruct((2, SWA_Q_HEADS, BLOCK, 2 * BLOCK), F32),
        compiler_params=_params(("parallel",)),
        name="swa_bias_table",
    )(bucket, rel_bias)


def _in_proj_kernel(x_ref, g_ref, w_ref, y_ref, h_ref):
    @pl.when(pl.program_id(1) == 0)
    def _():
        h_ref[...] = _rms(x_ref[...], g_ref[...], D_MODEL).astype(BF16)

    y_ref[...] = _dot(h_ref[...], w_ref[...]).astype(BF16)


def _in_proj(x2, gain, w, layer):
    t = x2.shape[0]
    tm, tn = 1024, IN_TILE_N
    return pl.pallas_call(
        _in_proj_kernel,
        grid=(t // tm, IN_W_PAD // tn),
        in_specs=[pl.BlockSpec((tm, D_MODEL), lambda i, j: (i, 0)),
                  pl.BlockSpec((1, D_MODEL), lambda i, j: (0, 0)),
                  pl.BlockSpec((None, D_MODEL, tn), lambda i, j: (layer, 0, j))],
        out_specs=pl.BlockSpec((tm, tn), lambda i, j: (i, j)),
        out_shape=jax.ShapeDtypeStruct((t, IN_W_PAD), BF16),
        scratch_shapes=[pltpu.VMEM((tm, D_MODEL), BF16)],
        compiler_params=_params(("parallel", "arbitrary")),
        name="in_proj",
    )(x2, gain, w)


def _swa_kernel(q_ref, kp_ref, kc_ref, vp_ref, vc_ref, bias_ref, qg_ref, kg_ref, sink_ref, o_ref):
    group = SWA_Q_HEADS // SWA_KV_HEADS
    first_variant = jnp.minimum(pl.program_id(1), 1)
    for hk in range(SWA_KV_HEADS):
        cols = slice(hk * HEAD_DIM, (hk + 1) * HEAD_DIM)
        k_all = jnp.concatenate([kp_ref[:, cols], kc_ref[:, cols]], axis=0).astype(F32)
        k_all = _rms(k_all, kg_ref[...], HEAD_DIM).astype(BF16)
        v_all = _with_ones(jnp.concatenate([vp_ref[:, cols], vc_ref[:, cols]], axis=0))
        for blk in range(SWA_BLOCKS_PER_STEP):
            rows = slice(blk * BLOCK, (blk + 1) * BLOCK)
            band = slice(blk * BLOCK, (blk + 2) * BLOCK)
            k, v = k_all[band], v_all[band]
            for g in range(group):
                h = hk * group + g
                hcols = slice(h * HEAD_DIM, (h + 1) * HEAD_DIM)
                q = _rms(q_ref[rows, hcols].astype(F32), qg_ref[...], HEAD_DIM).astype(BF16)
                bias = bias_ref[first_variant, h] if blk == 0 else bias_ref[1, h]
                s = _dot_t(q, k) + bias
                sink = sink_ref[0, h] * LOG2_E
                m = jnp.maximum(jnp.max(s, axis=-1, keepdims=True), sink)
                pv = _dot(jnp.exp2(s - m).astype(BF16), v)
                o = pv[:, :HEAD_DIM] / (pv[:, HEAD_DIM:] + jnp.exp2(sink - m))
                o_ref[rows, hcols] = o.astype(BF16)


def _swa_attention(y, bias, q_gain, k_gain, sinks, batch, seq):
    span = SWA_BLOCKS_PER_STEP * BLOCK
    steps = seq // span
    kcol = COL_KA // KVA
    vcol = COL_VA // KVA
    row = lambda b, n: b * steps + n
    prev = lambda b, n: (b * steps + n) * SWA_BLOCKS_PER_STEP - jnp.minimum(n, 1)
    return pl.pallas_call(
        _swa_kernel,
        grid=(batch, steps),
        in_specs=[pl.BlockSpec((span, WA), lambda b, n: (row(b, n), COL_QA // WA)),
                  pl.BlockSpec((BLOCK, KVA), lambda b, n: (prev(b, n), kcol)),
                  pl.BlockSpec((span, KVA), lambda b, n: (row(b, n), kcol)),
                  pl.BlockSpec((BLOCK, KVA), lambda b, n: (prev(b, n), vcol)),
                  pl.BlockSpec((span, KVA), lambda b, n: (row(b, n), vcol)),
                  _resident(bias.shape),
                  pl.BlockSpec((1, HEAD_DIM), lambda b, n: (0, 0)),
                  pl.BlockSpec((1, HEAD_DIM), lambda b, n: (0, 0)),
                  pl.BlockSpec(memory_space=pltpu.SMEM)],
        out_specs=pl.BlockSpec((span, WA), lambda b, n: (row(b, n), 0)),
        out_shape=jax.ShapeDtypeStruct((batch * seq, WA), BF16),
        compiler_params=_params(("parallel", "arbitrary")),
        name="swa_attention",
    )(y, y, y, y, y, bias, q_gain, k_gain, sinks)


def _sb_kernel(q_ref, k_ref, v_ref, o_ref, *, seq):
    r = lax.broadcasted_iota(jnp.int32, (ATT_TILE, ATT_TILE), 0)
    c = lax.broadcasted_iota(jnp.int32, (ATT_TILE, ATT_TILE), 1)
    later = jnp.where(r > c, 1.0, 0.0).astype(BF16)
    earlier = c < r
    sign_bit = jnp.uint32(0x80000000)
    blocks = [(i, j) for i in range(seq // ATT_TILE) for j in range(i, -1, -1)]
    z2s, own, betweens, runs, accs = {}, {}, {}, {}, {}

    def scores(n):
        i, j = blocks[n]
        q = q_ref[i * ATT_TILE:(i + 1) * ATT_TILE, :]
        z2s[n] = _dot_t(q, k_ref[j * ATT_TILE:(j + 1) * ATT_TILE, :])

    def drops(n):
        i, j = blocks[n]
        z2 = z2s.pop(n)
        neg_abs = pltpu.bitcast(pltpu.bitcast(z2, jnp.uint32) | sign_bit, F32)
        drop2 = jnp.maximum(z2, 0.0) + jnp.log2(1.0 + jnp.exp2(neg_abs))
        if j == i:
            drop2 = jnp.where(earlier, drop2, 0.0)
            runs[n] = jnp.zeros((ATT_TILE, 1), F32)
        if j > 0:
            runs[n + 1] = runs[n] + jnp.sum(drop2, axis=-1, keepdims=True)
        betweens[n] = _dot(drop2.astype(BF16), later)
        own[n] = z2 - drop2

    def weigh(n):
        i, j = blocks[n]
        w = jnp.exp2(own.pop(n) - betweens.pop(n) - runs.pop(n))
        if j == i:
            w = jnp.where(earlier, w, 0.0)
        pv = _dot(w.astype(BF16), v_ref[j * ATT_TILE:(j + 1) * ATT_TILE, :])
        accs[i] = pv if j == i else accs[i] + pv
        if j == 0:
            o_ref[i * ATT_TILE:(i + 1) * ATT_TILE, :] = accs.pop(i).astype(BF16)

    for t in range(len(blocks) + 2 * SB_SKEW):
        if t < len(blocks):
            scores(t)
        if 0 <= t - SB_SKEW < len(blocks):
            drops(t - SB_SKEW)
        if 0 <= t - 2 * SB_SKEW < len(blocks):
            weigh(t - 2 * SB_SKEW)


def _sb_attention(y, batch, seq):
    qc, kc, vc = COL_QB // HEAD_DIM, COL_KB // HEAD_DIM, COL_VB // HEAD_DIM
    return pl.pallas_call(
        functools.partial(_sb_kernel, seq=seq),
        grid=(batch, SB_HEADS),
        in_specs=[pl.BlockSpec((seq, HEAD_DIM), lambda b, h: (b, qc + h)),
                  pl.BlockSpec((seq, HEAD_DIM), lambda b, h: (b, kc + h)),
                  pl.BlockSpec((seq, HEAD_DIM), lambda b, h: (b, vc + h))],
        out_specs=pl.BlockSpec((seq, HEAD_DIM), lambda b, h: (b, h)),
        out_shape=jax.ShapeDtypeStruct((batch * seq, WB), BF16),
        compiler_params=_params(("parallel", "parallel")),
        name="sb_attention",
    )(y, y, y)


def _spread_rope(v):
    half = MLA_ROPE // 2
    zeros = jnp.zeros(v.shape[:-1] + (HEAD_DIM // 2 - half,), v.dtype)
    return jnp.concatenate([v[..., :half], zeros, v[..., half:], zeros], axis=-1)


def _rope(x, cos, sin_signed):
    return x * cos + pltpu.roll(x, HEAD_DIM // 2, 1) * sin_signed


def _mla_prep_kernel(cq_ref, ckv_ref, kr_ref, cos_ref, sin_ref, wuq_ref, wukv_ref,
                     cqg_ref, ckvg_ref, qgn_ref, qgr_ref, kgn_ref, kgr_ref,
                     q_ref, k_ref, v_ref):
    scale = MLA_QK ** -0.5 * math.log2(math.e)
    rope0 = MLA_HEADS * MLA_NOPE
    cos, sin = cos_ref[...], sin_ref[...]
    cqn = _rms(cq_ref[...].astype(F32), cqg_ref[...], MLA_Q_RANK).astype(BF16)
    ckvn = _rms(ckv_ref[...].astype(F32), ckvg_ref[...], MLA_KV_RANK).astype(BF16)
    qc = _dot(cqn, wuq_ref[...])
    kvc = _dot(ckvn, wukv_ref[...])
    k_rope = _rope(_rms(kr_ref[...].astype(F32), kgr_ref[...], MLA_ROPE), cos, sin).astype(BF16)
    for h in range(MLA_HEADS):
        nope = slice(h * MLA_NOPE, (h + 1) * MLA_NOPE)
        spread_rope = slice(rope0 + h * HEAD_DIM, rope0 + (h + 1) * HEAD_DIM)
        qn = _rms(qc[:, nope], qgn_ref[...] * scale, MLA_NOPE)
        qr = _rope(_rms(qc[:, spread_rope], qgr_ref[...] * scale, MLA_ROPE), cos, sin)
        q_ref[:, 2 * h * HEAD_DIM:(2 * h + 1) * HEAD_DIM] = qn.astype(BF16)
        q_ref[:, (2 * h + 1) * HEAD_DIM:(2 * h + 2) * HEAD_DIM] = qr.astype(BF16)
        kn = _rms(kvc[:, nope], kgn_ref[...], MLA_NOPE)
        k_ref[:, 2 * h * HEAD_DIM:(2 * h + 1) * HEAD_DIM] = kn.astype(BF16)
        k_ref[:, (2 * h + 1) * HEAD_DIM:(2 * h + 2) * HEAD_DIM] = k_rope
    v_ref[...] = kvc[:, rope0:].astype(BF16)


def _mla_prep(y, cos_t, sin_t, w_uq, w_ukv, layer, cq_gain, ckv_gain, q_gain, k_gain):
    t = y.shape[0]
    tm = 512
    row = lambda v: v.reshape(1, -1)
    gains = [row(cq_gain), row(ckv_gain),
             row(q_gain[:MLA_NOPE]), row(_spread_rope(q_gain[MLA_NOPE:])),
             row(k_gain[:MLA_NOPE]), row(_spread_rope(k_gain[MLA_NOPE:]))]
    wq = 2 * MLA_HEADS * HEAD_DIM
    return pl.pallas_call(
        _mla_prep_kernel,
        grid=(t // tm,),
        in_specs=[pl.BlockSpec((tm, MLA_Q_RANK), lambda i: (i, COL_CQ // MLA_Q_RANK)),
                  pl.BlockSpec((tm, MLA_KV_RANK), lambda i: (i, COL_CKV // MLA_KV_RANK)),
                  pl.BlockSpec((tm, HEAD_DIM), lambda i: (i, COL_KR // HEAD_DIM)),
                  pl.BlockSpec((tm, HEAD_DIM), lambda i: (i, 0)),
                  pl.BlockSpec((tm, HEAD_DIM), lambda i: (i, 0)),
                  _resident_layer(w_uq.shape, layer), _resident_layer(w_ukv.shape, layer)]
                 + [_resident(g.shape) for g in gains],
        out_specs=[pl.BlockSpec((tm, wq), lambda i: (i, 0)),
                   pl.BlockSpec((tm, wq), lambda i: (i, 0)),
                   pl.BlockSpec((tm, WC), lambda i: (i, 0))],
        out_shape=[jax.ShapeDtypeStruct((t, wq), BF16),
                   jax.ShapeDtypeStruct((t, wq), BF16),
                   jax.ShapeDtypeStruct((t, WC), BF16)],
        compiler_params=_params(("parallel",)),
        name="mla_prep",
    )(y, y, y, cos_t, sin_t, w_uq, w_ukv, *gains)


def _mla_kernel(q_ref, k_ref, v_ref, o_ref, *, seq):
    r = lax.broadcasted_iota(jnp.int32, (ATT_TILE, ATT_TILE), 0)
    c = lax.broadcasted_iota(jnp.int32, (ATT_TILE, ATT_TILE), 1)
    causal = c <= r
    tiles = seq // ATT_TILE
    scores = {}

    def score(i):
        q = q_ref[i * ATT_TILE:(i + 1) * ATT_TILE, :]
        s = [_dot_t(q, k_ref[j * ATT_TILE:(j + 1) * ATT_TILE, :]) for j in range(i + 1)]
        s[i] = jnp.where(causal, s[i], MASK_VALUE)
        scores[i] = s

    def attend(i):
        s = scores.pop(i)
        m = jnp.max(functools.reduce(jnp.maximum, s), axis=-1, keepdims=True)
        p_all = jnp.concatenate([jnp.exp2(sj - m).astype(BF16) for sj in s], axis=1)
        acc = _dot(p_all, _with_ones(v_ref[0:(i + 1) * ATT_TILE, :]))
        o_ref[i * ATT_TILE:(i + 1) * ATT_TILE, :] = (acc[:, :MLA_V] / acc[:, MLA_V:]).astype(BF16)

    order = list(range(tiles - 1, -1, -1))
    for t in range(tiles + 1):
        if t < tiles:
            score(order[t])
        if t >= 1:
            attend(order[t - 1])


def _mla_attention(qm, km, vm, batch, seq):
    wqk = 2 * HEAD_DIM
    return pl.pallas_call(
        functools.partial(_mla_kernel, seq=seq),
        grid=(batch, MLA_HEADS),
        in_specs=[pl.BlockSpec((seq, wqk), lambda b, h: (b, h)),
                  pl.BlockSpec((seq, wqk), lambda b, h: (b, h)),
                  pl.BlockSpec((seq, MLA_V), lambda b, h: (b, h))],
        out_specs=pl.BlockSpec((seq, MLA_V), lambda b, h: (b, h)),
        out_shape=jax.ShapeDtypeStruct((batch * seq, WC), BF16),
        compiler_params=_params(("parallel", "parallel")),
        name="mla_attention",
    )(qm, km, vm)


def _merge_kernel(oa_ref, ob_ref, oc_ref, ga_ref, gb_ref, gc_ref, x_ref, wbr_ref, wout_ref, o_ref):
    ma = _dot(oa_ref[...], wbr_ref[0:WA, :])
    mb = _dot(ob_ref[...], wbr_ref[WA:WA + WB, :])
    mc = _dot(oc_ref[...], wbr_ref[WA + WB:, :])
    merged = (jax.nn.sigmoid(ga_ref[...].astype(F32)) * ma
              + jax.nn.sigmoid(gb_ref[...].astype(F32)) * mb
              + jax.nn.sigmoid(gc_ref[...].astype(F32)) * mc)
    o_ref[...] = x_ref[...] + _dot(merged.astype(BF16), wout_ref[...])


def _merge(o_a, o_b, o_c, y, x2, w_branch, w_out, layer):
    t = x2.shape[0]
    tm = 512
    gcol = COL_GATES // D_MODEL
    return pl.pallas_call(
        _merge_kernel,
        grid=(t // tm,),
        in_specs=[pl.BlockSpec((tm, WA), lambda i: (i, 0)),
                  pl.BlockSpec((tm, WB), lambda i: (i, 0)),
                  pl.BlockSpec((tm, WC), lambda i: (i, 0)),
                  pl.BlockSpec((tm, D_MODEL), lambda i: (i, gcol)),
                  pl.BlockSpec((tm, D_MODEL), lambda i: (i, gcol + 1)),
                  pl.BlockSpec((tm, D_MODEL), lambda i: (i, gcol + 2)),
                  pl.BlockSpec((tm, D_MODEL), lambda i: (i, 0)),
                  _resident_layer(w_branch.shape, layer), _resident_layer(w_out.shape, layer)],
        out_specs=pl.BlockSpec((tm, D_MODEL), lambda i: (i, 0)),
        out_shape=jax.ShapeDtypeStruct((t, D_MODEL), F32),
        compiler_params=_params(("parallel",)),
        name="gated_merge",
    )(o_a, o_b, o_c, y, y, y, x2, w_branch, w_out)


def _mem_kv_kernel(mem_ref, g_ref, wkv_ref, kg_ref, k_ref, v_ref):
    memn = _rms(mem_ref[...], g_ref[...], D_MODEL).astype(BF16)
    kv = _dot(memn, wkv_ref[...])
    for h in range(XA_HEADS):
        cols = slice(h * HEAD_DIM, (h + 1) * HEAD_DIM)
        k_ref[:, cols] = _rms(kv[:, cols], kg_ref[...], HEAD_DIM).astype(BF16)
    v_ref[...] = kv[:, XA_W:].astype(BF16)


def _mem_kv(mem2, gain, w_kv, layer, k_gain, batch, mem_len):
    return pl.pallas_call(
        _mem_kv_kernel,
        grid=(batch,),
        in_specs=[pl.BlockSpec((mem_len, D_MODEL), lambda b: (b, 0)),
                  _resident(gain.shape), _resident_layer(w_kv.shape, layer), _resident(k_gain.shape)],
        out_specs=[pl.BlockSpec((mem_len, XA_W), lambda b: (b, 0)),
                   pl.BlockSpec((mem_len, XA_W), lambda b: (b, 0))],
        out_shape=[jax.ShapeDtypeStruct((batch * mem_len, XA_W), BF16)] * 2,
        compiler_params=_params(("parallel",)),
        name="mem_kv",
    )(mem2, gain, w_kv, k_gain)


def _xattn_kernel(x_ref, g_ref, wq_ref, qg_ref, k_ref, v_ref, wo_ref, o_ref):
    qg = qg_ref[...] * (HEAD_DIM ** -0.5 * LOG2_E)
    sub = x_ref.shape[0] // XA_SUBTILES
    stages = [{} for _ in range(4)]

    def project(n):
        rows = slice(n * sub, (n + 1) * sub)
        xn = _rms(x_ref[rows, :], g_ref[...], D_MODEL).astype(BF16)
        stages[0][n] = _dot(xn, wq_ref[...])

    def score(n):
        q = stages[0].pop(n)
        cols = [slice(h * HEAD_DIM, (h + 1) * HEAD_DIM) for h in range(XA_HEADS)]
        stages[1][n] = [_dot_t(_rms(q[:, c], qg, HEAD_DIM).astype(BF16), k_ref[:, c]) for c in cols]

    def attend(n):
        heads = []
        for h, s in enumerate(stages[1].pop(n)):
            p = jnp.exp2(s - jnp.max(s, axis=-1, keepdims=True))
            pv = _dot(p.astype(BF16), v_ref[:, h * HEAD_DIM:(h + 1) * HEAD_DIM])
            heads.append((pv / jnp.sum(p, axis=-1, keepdims=True)).astype(BF16))
        stages[2][n] = jnp.concatenate(heads, axis=-1)

    def output(n):
        rows = slice(n * sub, (n + 1) * sub)
        o_ref[rows, :] = x_ref[rows, :] + _dot(stages[2].pop(n), wo_ref[...])

    steps = [project, score, attend, output]
    for t in range(XA_SUBTILES + len(steps) - 1):
        for depth, step in enumerate(steps):
            if 0 <= t - depth < XA_SUBTILES:
                step(t - depth)


def _xattn(x2, gain, w_q, q_gain, k_mem, v_mem, w_o, layer, seq, mem_len):
    t = x2.shape[0]
    tm = 512
    per_batch = seq // tm
    return pl.pallas_call(
        _xattn_kernel,
        grid=(t // tm,),
        in_specs=[pl.BlockSpec((tm, D_MODEL), lambda i: (i, 0)),
                  _resident(gain.shape), _resident_layer(w_q.shape, layer), _resident(q_gain.shape),
                  pl.BlockSpec((mem_len, XA_W), lambda i: (i // per_batch, 0)),
                  pl.BlockSpec((mem_len, XA_W), lambda i: (i // per_batch, 0)),
                  _resident_layer(w_o.shape, layer)],
        out_specs=pl.BlockSpec((tm, D_MODEL), lambda i: (i, 0)),
        out_shape=jax.ShapeDtypeStruct((t, D_MODEL), F32),
        compiler_params=_params(("parallel",)),
        name="mem_xattn",
    )(x2, gain, w_q, q_gain, k_mem, v_mem, w_o)


def _ffn_kernel(x_ref, g_ref, wg_ref, wu_ref, wd_ref, o_ref, h_ref):
    @pl.when(pl.program_id(1) == 0)
    def _():
        x = x_ref[...]
        h_ref[...] = _rms(x, g_ref[...], D_MODEL).astype(BF16)
        o_ref[...] = x

    h = h_ref[...]
    gate = _dot(h, wg_ref[...])
    up = _dot(h, wu_ref[...])
    act = (gate * jax.nn.sigmoid(gate) * up).astype(BF16)
    o_ref[...] += _dot(act, wd_ref[...])


def _ffn(x2, gain, w_gu, w_down, layer):
    t = x2.shape[0]
    tm, tf = 1024, 512
    nf = D_FF // tf
    return pl.pallas_call(
        _ffn_kernel,
        grid=(t // tm, nf),
        in_specs=[pl.BlockSpec((tm, D_MODEL), lambda i, f: (i, 0)),
                  pl.BlockSpec((1, D_MODEL), lambda i, f: (0, 0)),
                  pl.BlockSpec((None, D_MODEL, tf), lambda i, f: (layer, 0, f)),
                  pl.BlockSpec((None, D_MODEL, tf), lambda i, f: (layer, 0, nf + f)),
                  pl.BlockSpec((None, tf, D_MODEL), lambda i, f: (layer, f, 0))],
        out_specs=pl.BlockSpec((tm, D_MODEL), lambda i, f: (i, 0)),
        out_shape=jax.ShapeDtypeStruct((t, D_MODEL), F32),
        scratch_shapes=[pltpu.VMEM((tm, D_MODEL), BF16)],
        compiler_params=_params(("parallel", "arbitrary")),
        name="swiglu_ffn",
    )(x2, gain, w_gu, w_gu, w_down)


def _in_proj_weight(w):
    kr0 = COL_GATES
    pad = jnp.zeros(w.shape[:-1] + (IN_W_PAD - COL_KR - HEAD_DIM,), BF16)
    pieces = [w[..., :COL_QB].astype(BF16),
              (w[..., COL_QB:COL_QB + WB] * SB_SCORE_SCALE).astype(BF16),
              w[..., COL_QB + WB:kr0].astype(BF16),
              w[..., kr0 + MLA_ROPE:].astype(BF16),
              _spread_rope(w[..., kr0:kr0 + MLA_ROPE]).astype(BF16),
              pad]
    return jnp.concatenate(pieces, axis=-1)


def _mla_uq_weight(w):
    lead = w.shape[:-1]
    w = w.reshape(lead + (MLA_HEADS, MLA_QK))
    nope = w[..., :MLA_NOPE].reshape(lead + (MLA_HEADS * MLA_NOPE,))
    rope = _spread_rope(w[..., MLA_NOPE:]).reshape(lead + (MLA_HEADS * HEAD_DIM,))
    return jnp.concatenate([nope, rope], axis=-1).astype(BF16)


def _split_heads_weight(w, heads, first):
    lead = w.shape[:-1]
    w = w.reshape(lead + (heads, -1))
    a = w[..., :first].reshape(lead + (-1,))
    b = w[..., first:].reshape(lead + (-1,))
    return jnp.concatenate([a, b], axis=-1).astype(BF16)


def kernel(x, mem, positions, rel_bias, norm_mix, w_in, swa_q_norm, swa_k_norm, swa_sinks,
           mla_cq_norm, mla_ckv_norm, mla_w_uq, mla_w_ukv, mla_q_norm, mla_k_norm,
           w_branch, w_out, norm_xa, norm_mem, xa_wq, xa_wkv, xa_q_norm, xa_k_norm, xa_wo,
           norm_ffn, ffn_w_gu, ffn_w_down):
    batch, seq, d = x.shape
    mem_len = mem.shape[1]
    depth = w_in.shape[0]
    assert d == D_MODEL and seq % 512 == 0 and (batch * seq) % 1024 == 0 and mem_len % 8 == 0
    row = lambda v: v.reshape(1, -1)

    x2 = x.reshape(batch * seq, d)
    mem2 = mem.reshape(batch * mem_len, d)
    cos_t, sin_t = _rope_tables(positions)
    swa_bias = _swa_bias_table(rel_bias)

    w_in_b = _in_proj_weight(w_in)
    w_uq_b = _mla_uq_weight(mla_w_uq)
    w_ukv_b = _split_heads_weight(mla_w_ukv, MLA_HEADS, MLA_NOPE)
    w_branch_b, w_out_b = w_branch.astype(BF16), w_out.astype(BF16)
    xa_wkv_b = _split_heads_weight(xa_wkv, XA_HEADS, HEAD_DIM)
    xa_wq_b, xa_wo_b = xa_wq.astype(BF16), xa_wo.astype(BF16)
    w_gu_b, w_down_b = ffn_w_gu.astype(BF16), ffn_w_down.astype(BF16)

    for l in range(depth):
        y = _in_proj(x2, row(norm_mix[l]), w_in_b, l)
        o_a = _swa_attention(y, swa_bias, row(swa_q_norm[l] * (HEAD_DIM ** -0.5 * LOG2_E)),
                             row(swa_k_norm[l]), row(swa_sinks[l]), batch, seq)
        o_b = _sb_attention(y, batch, seq)
        qm, km, vm = _mla_prep(y, cos_t, sin_t, w_uq_b, w_ukv_b, l,
                               mla_cq_norm[l], mla_ckv_norm[l], mla_q_norm[l], mla_k_norm[l])
        o_c = _mla_attention(qm, km, vm, batch, seq)
        x2 = _merge(o_a, o_b, o_c, y, x2, w_branch_b, w_out_b, l)

        k_mem, v_mem = _mem_kv(mem2, row(norm_mem[l]), xa_wkv_b, l, row(xa_k_norm[l]), batch, mem_len)
        x2 = _xattn(x2, row(norm_xa[l]), xa_wq_b, row(xa_q_norm[l]), k_mem, v_mem, xa_wo_b, l,
                    seq, mem_len)
        x2 = _ffn(x2, row(norm_ffn[l]), w_gu_b, w_down_b, l)
    return x2.reshape(batch, seq, d)
```

```python
import functools
import math

import jax
import jax.numpy as jnp
from jax import lax
from jax.experimental import pallas as pl
from jax.experimental.pallas import tpu as pltpu

F32 = jnp.float32
BF16 = jnp.bfloat16

D_MODEL = 2048
HEAD_DIM = 128
BLOCK = 128
SWA_Q_HEADS = 8
SWA_KV_HEADS = 2
WINDOW = 128
SB_HEADS = 4
MLA_HEADS = 4
MLA_Q_RANK = 512
MLA_KV_RANK = 512
MLA_NOPE = 128
MLA_ROPE = 64
MLA_V = 128
ROPE_THETA = 10000.0
XA_HEADS = 4
REL_BUCKETS = 32
REL_MAX_DIST = 128
D_FF = ((8 * D_MODEL + 3 * 256 - 1) // (3 * 256)) * 256
EPS = 1e-6

WA = SWA_Q_HEADS * HEAD_DIM
KVA = SWA_KV_HEADS * HEAD_DIM
WB = SB_HEADS * HEAD_DIM
WC = MLA_HEADS * MLA_V
XA_W = XA_HEADS * HEAD_DIM
MLA_QK = MLA_NOPE + MLA_ROPE

COL_QA = 0
COL_KA = COL_QA + WA
COL_VA = COL_KA + KVA
COL_QB = COL_VA + KVA
COL_KB = COL_QB + WB
COL_VB = COL_KB + WB
COL_CQ = COL_VB + WB
COL_CKV = COL_CQ + MLA_Q_RANK
COL_GATES = COL_CKV + MLA_KV_RANK
IN_W = COL_GATES + 3 * D_MODEL
IN_TILE_N = 2048

SWA_BLOCKS_PER_STEP = 4
LOG2_E = math.log2(math.e)
SB_SCORE_SCALE = HEAD_DIM ** -0.5 * LOG2_E
SB_SKEW = 1
XA_SUBTILES = 2
ATT_TILE = 256
MASK_VALUE = -1e30
V7X_VMEM_LIMIT_BYTES = 56 * 1024 * 1024


def _params(semantics, flags=None):
    return pltpu.CompilerParams(dimension_semantics=semantics,
                                vmem_limit_bytes=V7X_VMEM_LIMIT_BYTES, flags=flags)


def _rms(x, gain, width):
    ms = jnp.sum(x * x, axis=-1, keepdims=True) * (1.0 / width)
    return x * lax.rsqrt(ms + EPS) * gain


def _dot(a, b):
    return jnp.dot(a, b, preferred_element_type=F32)


def _dot_t(a, b):
    return lax.dot_general(a, b, (((1,), (1,)), ((), ())), preferred_element_type=F32)


def _with_ones(v):
    return jnp.concatenate([v, jnp.ones(v.shape, v.dtype)], axis=1)


def _resident(shape):
    nd = len(shape)
    return pl.BlockSpec(shape, lambda *_: (0,) * nd, pipeline_mode=pl.Buffered(1))


def _resident_layer(stacked_shape, layer):
    nd = len(stacked_shape) - 1
    return pl.BlockSpec((None,) + tuple(stacked_shape[1:]), lambda *_: (layer,) + (0,) * nd,
                        pipeline_mode=pl.Buffered(1))


def _rope_table_kernel(pos_ref, inv_ref, cos_ref, sin_ref):
    ang = pos_ref[...].astype(F32) * inv_ref[...]
    lane = lax.broadcasted_iota(jnp.int32, ang.shape, 1)
    cos_ref[...] = jnp.cos(ang)
    s = jnp.sin(ang)
    sin_ref[...] = jnp.where(lane < HEAD_DIM // 2, -s, s)


def _rope_tables(positions):
    t = positions.size
    half = MLA_ROPE // 2
    inv = ROPE_THETA ** (-jnp.arange(half, dtype=F32) / half)
    inv_l = jnp.tile(inv, HEAD_DIM // half).reshape(1, HEAD_DIM)
    pos_l = jnp.broadcast_to(positions.reshape(t, 1), (t, HEAD_DIM))
    tm = 1024
    return pl.pallas_call(
        _rope_table_kernel,
        grid=(t // tm,),
        in_specs=[pl.BlockSpec((tm, HEAD_DIM), lambda i: (i, 0)),
                  pl.BlockSpec((1, HEAD_DIM), lambda i: (0, 0))],
        out_specs=[pl.BlockSpec((tm, HEAD_DIM), lambda i: (i, 0)),
                   pl.BlockSpec((tm, HEAD_DIM), lambda i: (i, 0))],
        out_shape=[jax.ShapeDtypeStruct((t, HEAD_DIM), F32)] * 2,
        compiler_params=_params(("parallel",)),
        name="rope_tables",
    )(pos_l, inv_l)


def _t5_bucket(rel):
    n = jnp.maximum(rel, 0)
    exact = REL_BUCKETS // 2
    nf = jnp.maximum(n, exact).astype(F32)
    large = exact + (jnp.log(nf / exact) / math.log(REL_MAX_DIST / exact)
                     * (REL_BUCKETS - exact)).astype(jnp.int32)
    large = jnp.minimum(large, REL_BUCKETS - 1)
    return jnp.where(n < exact, n, large)


def _swa_bias_kernel(bucket_ref, rel_bias_ref, out_ref):
    h = pl.program_id(0)
    bucket = bucket_ref[...]
    bias = jnp.zeros(bucket.shape, F32)
    for b in range(REL_BUCKETS):
        bias = jnp.where(bucket == b, rel_bias_ref[b, h] * LOG2_E, bias)
    i = lax.broadcasted_iota(jnp.int32, bucket.shape, 0)
    j = lax.broadcasted_iota(jnp.int32, bucket.shape, 1)
    rel = BLOCK + i - j
    in_window = (rel >= 0) & (rel < WINDOW)
    out_ref[0, 0] = jnp.where(in_window & (j >= BLOCK), bias, MASK_VALUE)
    out_ref[1, 0] = jnp.where(in_window, bias, MASK_VALUE)


def _swa_bias_table(rel_bias):
    i = jnp.arange(BLOCK)[:, None]
    j = jnp.arange(2 * BLOCK)[None, :]
    bucket = _t5_bucket(BLOCK + i - j).astype(jnp.int32)
    return pl.pallas_call(
        _swa_bias_kernel,
        grid=(SWA_Q_HEADS,),
        in_specs=[pl.BlockSpec((BLOCK, 2 * BLOCK), lambda h: (0, 0)),
                  pl.BlockSpec(memory_space=pltpu.SMEM)],
        out_specs=pl.BlockSpec((2, 1, BLOCK, 2 * BLOCK), lambda h: (0, h, 0, 0)),
        out_shape=jax.ShapeDtypeStruct((2, SWA_Q_HEADS, BLOCK, 2 * BLOCK), F32),
        compiler_params=_params(("parallel",)),
        name="swa_bias_table",
    )(bucket, rel_bias)


def _inv_rms(x):
    return lax.rsqrt(jnp.mean(x * x, axis=-1, keepdims=True) + EPS)


def _in_proj_kernel(x_ref, w_ref, wkr_ref, y_ref, kr_ref, xb_ref, r_ref):
    @pl.when(pl.program_id(1) == 0)
    def _():
        x = x_ref[...]
        r = _inv_rms(x)
        xb = x.astype(BF16)
        xb_ref[...] = xb
        r_ref[...] = r
        kr_ref[...] = (_dot(xb, wkr_ref[...]) * r).astype(BF16)

    y_ref[...] = (_dot(xb_ref[...], w_ref[...]) * r_ref[...]).astype(BF16)


def _in_proj(x2, w, w_kr, layer):
    t = x2.shape[0]
    tm, tn = 1024, IN_TILE_N
    return pl.pallas_call(
        _in_proj_kernel,
        grid=(t // tm, IN_W // tn),
        in_specs=[pl.BlockSpec((tm, D_MODEL), lambda i, j: (i, 0)),
                  pl.BlockSpec((None, D_MODEL, tn), lambda i, j: (layer, 0, j)),
                  _resident_layer(w_kr.shape, layer)],
        out_specs=[pl.BlockSpec((tm, tn), lambda i, j: (i, j)),
                   pl.BlockSpec((tm, HEAD_DIM), lambda i, j: (i, 0))],
        out_shape=[jax.ShapeDtypeStruct((t, IN_W), BF16),
                   jax.ShapeDtypeStruct((t, HEAD_DIM), BF16)],
        scratch_shapes=[pltpu.VMEM((tm, D_MODEL), BF16), pltpu.VMEM((tm, 1), F32)],
        compiler_params=_params(("parallel", "arbitrary")),
        name="in_proj",
    )(x2, w, w_kr)


def _swa_kernel(q_ref, kp_ref, kc_ref, vp_ref, vc_ref, bias_ref, qg_ref, kg_ref, sink_ref, o_ref):
    group = SWA_Q_HEADS // SWA_KV_HEADS
    first_variant = jnp.minimum(pl.program_id(1), 1)
    for hk in range(SWA_KV_HEADS):
        cols = slice(hk * HEAD_DIM, (hk + 1) * HEAD_DIM)
        k_all = jnp.concatenate([kp_ref[:, cols], kc_ref[:, cols]], axis=0).astype(F32)
        k_all = _rms(k_all, kg_ref[...], HEAD_DIM).astype(BF16)
        v_all = _with_ones(jnp.concatenate([vp_ref[:, cols], vc_ref[:, cols]], axis=0))
        for blk in range(SWA_BLOCKS_PER_STEP):
            rows = slice(blk * BLOCK, (blk + 1) * BLOCK)
            band = slice(blk * BLOCK, (blk + 2) * BLOCK)
            k, v = k_all[band], v_all[band]
            for g in range(group):
                h = hk * group + g
                hcols = slice(h * HEAD_DIM, (h + 1) * HEAD_DIM)
                q = _rms(q_ref[rows, hcols].astype(F32), qg_ref[...], HEAD_DIM).astype(BF16)
                bias = bias_ref[first_variant, h] if blk == 0 else bias_ref[1, h]
                s = _dot_t(q, k) + bias
                sink = sink_ref[0, h] * LOG2_E
                m = jnp.maximum(jnp.max(s, axis=-1, keepdims=True), sink)
                pv = _dot(jnp.exp2(s - m).astype(BF16), v)
                o = pv[:, :HEAD_DIM] / (pv[:, HEAD_DIM:] + jnp.exp2(sink - m))
                o_ref[rows, hcols] = o.astype(BF16)


def _swa_attention(y, bias, q_gain, k_gain, sinks, batch, seq):
    span = SWA_BLOCKS_PER_STEP * BLOCK
    steps = seq // span
    kcol = COL_KA // KVA
    vcol = COL_VA // KVA
    row = lambda b, n: b * steps + n
    prev = lambda b, n: (b * steps + n) * SWA_BLOCKS_PER_STEP - jnp.minimum(n, 1)
    return pl.pallas_call(
        _swa_kernel,
        grid=(batch, steps),
        in_specs=[pl.BlockSpec((span, WA), lambda b, n: (row(b, n), COL_QA // WA)),
                  pl.BlockSpec((BLOCK, KVA), lambda b, n: (prev(b, n), kcol)),
                  pl.BlockSpec((span, KVA), lambda b, n: (row(b, n), kcol)),
                  pl.BlockSpec((BLOCK, KVA), lambda b, n: (prev(b, n), vcol)),
                  pl.BlockSpec((span, KVA), lambda b, n: (row(b, n), vcol)),
                  _resident(bias.shape),
                  pl.BlockSpec((1, HEAD_DIM), lambda b, n: (0, 0)),
                  pl.BlockSpec((1, HEAD_DIM), lambda b, n: (0, 0)),
                  pl.BlockSpec(memory_space=pltpu.SMEM)],
        out_specs=pl.BlockSpec((span, WA), lambda b, n: (row(b, n), 0)),
        out_shape=jax.ShapeDtypeStruct((batch * seq, WA), BF16),
        compiler_params=_params(("parallel", "arbitrary")),
        name="swa_attention",
    )(y, y, y, y, y, bias, q_gain, k_gain, sinks)


def _sb_kernel(q_ref, k_ref, v_ref, o_ref, *, seq):
    r = lax.broadcasted_iota(jnp.int32, (ATT_TILE, ATT_TILE), 0)
    c = lax.broadcasted_iota(jnp.int32, (ATT_TILE, ATT_TILE), 1)
    later = jnp.where(r > c, 1.0, 0.0).astype(BF16)
    earlier = c < r
    sign_bit = jnp.uint32(0x80000000)
    blocks = [(i, j) for i in range(seq // ATT_TILE) for j in range(i, -1, -1)]
    z2s, own, betweens, runs, accs = {}, {}, {}, {}, {}

    def scores(n):
        i, j = blocks[n]
        q = q_ref[i * ATT_TILE:(i + 1) * ATT_TILE, :]
        z2s[n] = _dot_t(q, k_ref[j * ATT_TILE:(j + 1) * ATT_TILE, :])

    def drops(n):
        i, j = blocks[n]
        z2 = z2s.pop(n)
        neg_abs = pltpu.bitcast(pltpu.bitcast(z2, jnp.uint32) | sign_bit, F32)
        drop2 = jnp.maximum(z2, 0.0) + jnp.log2(1.0 + jnp.exp2(neg_abs))
        if j == i:
            drop2 = jnp.where(earlier, drop2, 0.0)
            runs[n] = jnp.zeros((ATT_TILE, 1), F32)
        if j > 0:
            runs[n + 1] = runs[n] + jnp.sum(drop2, axis=-1, keepdims=True)
        betweens[n] = _dot(drop2.astype(BF16), later)
        own[n] = z2 - drop2

    def weigh(n):
        i, j = blocks[n]
        w = jnp.exp2(own.pop(n) - betweens.pop(n) - runs.pop(n))
        if j == i:
            w = jnp.where(earlier, w, 0.0)
        pv = _dot(w.astype(BF16), v_ref[j * ATT_TILE:(j + 1) * ATT_TILE, :])
        accs[i] = pv if j == i else accs[i] + pv
        if j == 0:
            o_ref[i * ATT_TILE:(i + 1) * ATT_TILE, :] = accs.pop(i).astype(BF16)

    for t in range(len(blocks) + 2 * SB_SKEW):
        if t < len(blocks):
            scores(t)
        if 0 <= t - SB_SKEW < len(blocks):
            drops(t - SB_SKEW)
        if 0 <= t - 2 * SB_SKEW < len(blocks):
            weigh(t - 2 * SB_SKEW)


def _sb_attention(y, batch, seq):
    qc, kc, vc = COL_QB // HEAD_DIM, COL_KB // HEAD_DIM, COL_VB // HEAD_DIM
    return pl.pallas_call(
        functools.partial(_sb_kernel, seq=seq),
        grid=(batch, SB_HEADS),
        in_specs=[pl.BlockSpec((seq, HEAD_DIM), lambda b, h: (b, qc + h)),
                  pl.BlockSpec((seq, HEAD_DIM), lambda b, h: (b, kc + h)),
                  pl.BlockSpec((seq, HEAD_DIM), lambda b, h: (b, vc + h))],
        out_specs=pl.BlockSpec((seq, HEAD_DIM), lambda b, h: (b, h)),
        out_shape=jax.ShapeDtypeStruct((batch * seq, WB), BF16),
        compiler_params=_params(("parallel", "parallel")),
        name="sb_attention",
    )(y, y, y)


def _spread_rope(v):
    half = MLA_ROPE // 2
    zeros = jnp.zeros(v.shape[:-1] + (HEAD_DIM // 2 - half,), v.dtype)
    return jnp.concatenate([v[..., :half], zeros, v[..., half:], zeros], axis=-1)


def _rope(x, cos, sin_signed):
    return x * cos + pltpu.roll(x, HEAD_DIM // 2, 1) * sin_signed


def _mla_prep_kernel(cq_ref, ckv_ref, kr_ref, cos_ref, sin_ref, wuq_ref, wukv_ref,
                     cqg_ref, ckvg_ref, qgn_ref, qgr_ref, kgn_ref, kgr_ref,
                     q_ref, k_ref, v_ref):
    scale = MLA_QK ** -0.5 * math.log2(math.e)
    rope0 = MLA_HEADS * MLA_NOPE
    cos, sin = cos_ref[...], sin_ref[...]
    cqn = _rms(cq_ref[...].astype(F32), cqg_ref[...], MLA_Q_RANK).astype(BF16)
    ckvn = _rms(ckv_ref[...].astype(F32), ckvg_ref[...], MLA_KV_RANK).astype(BF16)
    qc = _dot(cqn, wuq_ref[...])
    kvc = _dot(ckvn, wukv_ref[...])
    k_rope = _rope(_rms(kr_ref[...].astype(F32), kgr_ref[...], MLA_ROPE), cos, sin).astype(BF16)
    for h in range(MLA_HEADS):
        nope = slice(h * MLA_NOPE, (h + 1) * MLA_NOPE)
        spread_rope = slice(rope0 + h * HEAD_DIM, rope0 + (h + 1) * HEAD_DIM)
        qn = _rms(qc[:, nope], qgn_ref[...] * scale, MLA_NOPE)
        qr = _rope(_rms(qc[:, spread_rope], qgr_ref[...] * scale, MLA_ROPE), cos, sin)
        q_ref[:, 2 * h * HEAD_DIM:(2 * h + 1) * HEAD_DIM] = qn.astype(BF16)
        q_ref[:, (2 * h + 1) * HEAD_DIM:(2 * h + 2) * HEAD_DIM] = qr.astype(BF16)
        kn = _rms(kvc[:, nope], kgn_ref[...], MLA_NOPE)
        k_ref[:, 2 * h * HEAD_DIM:(2 * h + 1) * HEAD_DIM] = kn.astype(BF16)
        k_ref[:, (2 * h + 1) * HEAD_DIM:(2 * h + 2) * HEAD_DIM] = k_rope
    v_ref[...] = kvc[:, rope0:].astype(BF16)


def _mla_prep(y, kr, cos_t, sin_t, w_uq, w_ukv, layer, cq_gain, ckv_gain, q_gain, k_gain):
    t = y.shape[0]
    tm = 512
    row = lambda v: v.reshape(1, -1)
    gains = [row(cq_gain), row(ckv_gain),
             row(q_gain[:MLA_NOPE]), row(_spread_rope(q_gain[MLA_NOPE:])),
             row(k_gain[:MLA_NOPE]), row(_spread_rope(k_gain[MLA_NOPE:]))]
    wq = 2 * MLA_HEADS * HEAD_DIM
    return pl.pallas_call(
        _mla_prep_kernel,
        grid=(t // tm,),
        in_specs=[pl.BlockSpec((tm, MLA_Q_RANK), lambda i: (i, COL_CQ // MLA_Q_RANK)),
                  pl.BlockSpec((tm, MLA_KV_RANK), lambda i: (i, COL_CKV // MLA_KV_RANK)),
                  pl.BlockSpec((tm, HEAD_DIM), lambda i: (i, 0)),
                  pl.BlockSpec((tm, HEAD_DIM), lambda i: (i, 0)),
                  pl.BlockSpec((tm, HEAD_DIM), lambda i: (i, 0)),
                  _resident_layer(w_uq.shape, layer), _resident_layer(w_ukv.shape, layer)]
                 + [_resident(g.shape) for g in gains],
        out_specs=[pl.BlockSpec((tm, wq), lambda i: (i, 0)),
                   pl.BlockSpec((tm, wq), lambda i: (i, 0)),
                   pl.BlockSpec((tm, WC), lambda i: (i, 0))],
        out_shape=[jax.ShapeDtypeStruct((t, wq), BF16),
                   jax.ShapeDtypeStruct((t, wq), BF16),
                   jax.ShapeDtypeStruct((t, WC), BF16)],
        compiler_params=_params(("parallel",)),
        name="mla_prep",
    )(y, y, kr, cos_t, sin_t, w_uq, w_ukv, *gains)


def _mla_kernel(q_ref, k_ref, v_ref, o_ref, *, seq):
    r = lax.broadcasted_iota(jnp.int32, (ATT_TILE, ATT_TILE), 0)
    c = lax.broadcasted_iota(jnp.int32, (ATT_TILE, ATT_TILE), 1)
    causal = c <= r
    tiles = seq // ATT_TILE
    scores = {}

    def score(i):
        q = q_ref[i * ATT_TILE:(i + 1) * ATT_TILE, :]
        s = [_dot_t(q, k_ref[j * ATT_TILE:(j + 1) * ATT_TILE, :]) for j in range(i + 1)]
        s[i] = jnp.where(causal, s[i], MASK_VALUE)
        scores[i] = s

    def attend(i):
        s = scores.pop(i)
        m = jnp.max(functools.reduce(jnp.maximum, s), axis=-1, keepdims=True)
        p_all = jnp.concatenate([jnp.exp2(sj - m).astype(BF16) for sj in s], axis=1)
        acc = _dot(p_all, _with_ones(v_ref[0:(i + 1) * ATT_TILE, :]))
        o_ref[i * ATT_TILE:(i + 1) * ATT_TILE, :] = (acc[:, :MLA_V] / acc[:, MLA_V:]).astype(BF16)

    order = list(range(tiles - 1, -1, -1))
    for t in range(tiles + 1):
        if t < tiles:
            score(order[t])
        if t >= 1:
            attend(order[t - 1])


def _mla_attention(qm, km, vm, batch, seq):
    wqk = 2 * HEAD_DIM
    return pl.pallas_call(
        functools.partial(_mla_kernel, seq=seq),
        grid=(batch, MLA_HEADS),
        in_specs=[pl.BlockSpec((seq, wqk), lambda b, h: (b, h)),
                  pl.BlockSpec((seq, wqk), lambda b, h: (b, h)),
                  pl.BlockSpec((seq, MLA_V), lambda b, h: (b, h))],
        out_specs=pl.BlockSpec((seq, MLA_V), lambda b, h: (b, h)),
        out_shape=jax.ShapeDtypeStruct((batch * seq, WC), BF16),
        compiler_params=_params(("parallel", "parallel")),
        name="mla_attention",
    )(qm, km, vm)


def _merge_kernel(oa_ref, ob_ref, oc_ref, ga_ref, gb_ref, gc_ref, x_ref, wbr_ref, wout_ref, o_ref):
    ma = _dot(oa_ref[...], wbr_ref[0:WA, :])
    mb = _dot(ob_ref[...], wbr_ref[WA:WA + WB, :])
    mc = _dot(oc_ref[...], wbr_ref[WA + WB:, :])
    merged = (jax.nn.sigmoid(ga_ref[...].astype(F32)) * ma
              + jax.nn.sigmoid(gb_ref[...].astype(F32)) * mb
              + jax.nn.sigmoid(gc_ref[...].astype(F32)) * mc)
    o_ref[...] = x_ref[...] + _dot(merged.astype(BF16), wout_ref[...])


def _merge(o_a, o_b, o_c, y, x2, w_branch, w_out, layer):
    t = x2.shape[0]
    tm = 512
    gcol = COL_GATES // D_MODEL
    return pl.pallas_call(
        _merge_kernel,
        grid=(t // tm,),
        in_specs=[pl.BlockSpec((tm, WA), lambda i: (i, 0)),
                  pl.BlockSpec((tm, WB), lambda i: (i, 0)),
                  pl.BlockSpec((tm, WC), lambda i: (i, 0)),
                  pl.BlockSpec((tm, D_MODEL), lambda i: (i, gcol)),
                  pl.BlockSpec((tm, D_MODEL), lambda i: (i, gcol + 1)),
                  pl.BlockSpec((tm, D_MODEL), lambda i: (i, gcol + 2)),
                  pl.BlockSpec((tm, D_MODEL), lambda i: (i, 0)),
                  _resident_layer(w_branch.shape, layer), _resident_layer(w_out.shape, layer)],
        out_specs=pl.BlockSpec((tm, D_MODEL), lambda i: (i, 0)),
        out_shape=jax.ShapeDtypeStruct((t, D_MODEL), F32),
        compiler_params=_params(("parallel",)),
        name="gated_merge",
    )(o_a, o_b, o_c, y, y, y, x2, w_branch, w_out)


def _mem_kv_kernel(mem_ref, g_ref, wkv_ref, kg_ref, k_ref, v_ref):
    memn = _rms(mem_ref[...], g_ref[...], D_MODEL).astype(BF16)
    kv = _dot(memn, wkv_ref[...])
    for h in range(XA_HEADS):
        cols = slice(h * HEAD_DIM, (h + 1) * HEAD_DIM)
        k_ref[:, cols] = _rms(kv[:, cols], kg_ref[...], HEAD_DIM).astype(BF16)
    v_ref[...] = kv[:, XA_W:].astype(BF16)


def _mem_kv(mem2, gain, w_kv, layer, k_gain, batch, mem_len):
    return pl.pallas_call(
        _mem_kv_kernel,
        grid=(batch,),
        in_specs=[pl.BlockSpec((mem_len, D_MODEL), lambda b: (b, 0)),
                  _resident(gain.shape), _resident_layer(w_kv.shape, layer), _resident(k_gain.shape)],
        out_specs=[pl.BlockSpec((mem_len, XA_W), lambda b: (b, 0)),
                   pl.BlockSpec((mem_len, XA_W), lambda b: (b, 0))],
        out_shape=[jax.ShapeDtypeStruct((batch * mem_len, XA_W), BF16)] * 2,
        compiler_params=_params(("parallel",)),
        name="mem_kv",
    )(mem2, gain, w_kv, k_gain)


def _xattn_kernel(x_ref, g_ref, wq_ref, qg_ref, k_ref, v_ref, wo_ref, o_ref):
    qg = qg_ref[...] * (HEAD_DIM ** -0.5 * LOG2_E)
    sub = x_ref.shape[0] // XA_SUBTILES
    stages = [{} for _ in range(4)]

    def project(n):
        rows = slice(n * sub, (n + 1) * sub)
        xn = _rms(x_ref[rows, :], g_ref[...], D_MODEL).astype(BF16)
        stages[0][n] = _dot(xn, wq_ref[...])

    def score(n):
        q = stages[0].pop(n)
        cols = [slice(h * HEAD_DIM, (h + 1) * HEAD_DIM) for h in range(XA_HEADS)]
        stages[1][n] = [_dot_t(_rms(q[:, c], qg, HEAD_DIM).astype(BF16), k_ref[:, c]) for c in cols]

    def attend(n):
        heads = []
        for h, s in enumerate(stages[1].pop(n)):
            p = jnp.exp2(s - jnp.max(s, axis=-1, keepdims=True))
            pv = _dot(p.astype(BF16), v_ref[:, h * HEAD_DIM:(h + 1) * HEAD_DIM])
            heads.append((pv / jnp.sum(p, axis=-1, keepdims=True)).astype(BF16))
        stages[2][n] = jnp.concatenate(heads, axis=-1)

    def output(n):
        rows = slice(n * sub, (n + 1) * sub)
        o_ref[rows, :] = x_ref[rows, :] + _dot(stages[2].pop(n), wo_ref[...])

    steps = [project, score, attend, output]
    for t in range(XA_SUBTILES + len(steps) - 1):
        for depth, step in enumerate(steps):
            if 0 <= t - depth < XA_SUBTILES:
                step(t - depth)


def _xattn(x2, gain, w_q, q_gain, k_mem, v_mem, w_o, layer, seq, mem_len):
    t = x2.shape[0]
    tm = 512
    per_batch = seq // tm
    return pl.pallas_call(
        _xattn_kernel,
        grid=(t // tm,),
        in_specs=[pl.BlockSpec((tm, D_MODEL), lambda i: (i, 0)),
                  _resident(gain.shape), _resident_layer(w_q.shape, layer), _resident(q_gain.shape),
                  pl.BlockSpec((mem_len, XA_W), lambda i: (i // per_batch, 0)),
                  pl.BlockSpec((mem_len, XA_W), lambda i: (i // per_batch, 0)),
                  _resident_layer(w_o.shape, layer)],
        out_specs=pl.BlockSpec((tm, D_MODEL), lambda i: (i, 0)),
        out_shape=jax.ShapeDtypeStruct((t, D_MODEL), F32),
        compiler_params=_params(("parallel",)),
        name="mem_xattn",
    )(x2, gain, w_q, q_gain, k_mem, v_mem, w_o)


def _ffn_kernel(x_ref, wg_ref, wu_ref, wd_ref, o_ref, xb_ref, r_ref):
    @pl.when(pl.program_id(1) == 0)
    def _():
        x = x_ref[...]
        xb_ref[...] = x.astype(BF16)
        r_ref[...] = _inv_rms(x)
        o_ref[...] = x

    xb, r = xb_ref[...], r_ref[...]
    gate = _dot(xb, wg_ref[...]) * r
    up = _dot(xb, wu_ref[...]) * r
    act = (gate * jax.nn.sigmoid(gate) * up).astype(BF16)
    o_ref[...] += _dot(act, wd_ref[...])


def _ffn(x2, w_gu, w_down, layer):
    t = x2.shape[0]
    tm, tf = 1024, 512
    nf = D_FF // tf
    return pl.pallas_call(
        _ffn_kernel,
        grid=(t // tm, nf),
        in_specs=[pl.BlockSpec((tm, D_MODEL), lambda i, f: (i, 0)),
                  pl.BlockSpec((None, D_MODEL, tf), lambda i, f: (layer, 0, f)),
                  pl.BlockSpec((None, D_MODEL, tf), lambda i, f: (layer, 0, nf + f)),
                  pl.BlockSpec((None, tf, D_MODEL), lambda i, f: (layer, f, 0))],
        out_specs=pl.BlockSpec((tm, D_MODEL), lambda i, f: (i, 0)),
        out_shape=jax.ShapeDtypeStruct((t, D_MODEL), F32),
        scratch_shapes=[pltpu.VMEM((tm, D_MODEL), BF16), pltpu.VMEM((tm, 1), F32)],
        compiler_params=_params(("parallel", "arbitrary")),
        name="swiglu_ffn",
    )(x2, w_gu, w_gu, w_down)


def _in_proj_weight_kernel(w_ref, g_ref, o_ref, okr_ref):
    kr0 = COL_GATES
    half = MLA_ROPE // 2
    g = g_ref[...]
    piece = lambda a, b, s: (w_ref[:, a:b] * s).astype(BF16)
    o_ref[:, :COL_QB] = piece(0, COL_QB, g)
    o_ref[:, COL_QB:COL_QB + WB] = piece(COL_QB, COL_QB + WB, g * SB_SCORE_SCALE)
    o_ref[:, COL_QB + WB:kr0] = piece(COL_QB + WB, kr0, g)
    o_ref[:, COL_GATES:] = piece(kr0 + MLA_ROPE, kr0 + MLA_ROPE + 3 * D_MODEL, g)
    gap = jnp.zeros((w_ref.shape[0], HEAD_DIM // 2 - half), BF16)
    okr_ref[...] = jnp.concatenate([piece(kr0, kr0 + half, g), gap,
                                    piece(kr0 + half, kr0 + MLA_ROPE, g), gap], axis=1)


def _in_proj_weight(w, gain):
    depth, d, width = w.shape
    tr = 256
    return pl.pallas_call(
        _in_proj_weight_kernel,
        grid=(depth, d // tr),
        in_specs=[pl.BlockSpec((None, tr, width), lambda l, i: (l, i, 0)),
                  pl.BlockSpec((None, tr, 1), lambda l, i: (l, i, 0))],
        out_specs=[pl.BlockSpec((None, tr, IN_W), lambda l, i: (l, i, 0)),
                   pl.BlockSpec((None, tr, HEAD_DIM), lambda l, i: (l, i, 0))],
        out_shape=[jax.ShapeDtypeStruct((depth, d, IN_W), BF16),
                   jax.ShapeDtypeStruct((depth, d, HEAD_DIM), BF16)],
        compiler_params=_params(("parallel", "parallel")),
        name="in_proj_weight_layout",
    )(w, gain.reshape(depth, d, 1))


def _mla_uq_weight(w):
    lead = w.shape[:-1]
    w = w.reshape(lead + (MLA_HEADS, MLA_QK))
    nope = w[..., :MLA_NOPE].reshape(lead + (MLA_HEADS * MLA_NOPE,))
    rope = _spread_rope(w[..., MLA_NOPE:]).reshape(lead + (MLA_HEADS * HEAD_DIM,))
    return jnp.concatenate([nope, rope], axis=-1).astype(BF16)


def _split_heads_weight(w, heads, first):
    lead = w.shape[:-1]
    w = w.reshape(lead + (heads, -1))
    a = w[..., :first].reshape(lead + (-1,))
    b = w[..., first:].reshape(lead + (-1,))
    return jnp.concatenate([a, b], axis=-1).astype(BF16)


def kernel(x, mem, positions, rel_bias, norm_mix, w_in, swa_q_norm, swa_k_norm, swa_sinks,
           mla_cq_norm, mla_ckv_norm, mla_w_uq, mla_w_ukv, mla_q_norm, mla_k_norm,
           w_branch, w_out, norm_xa, norm_mem, xa_wq, xa_wkv, xa_q_norm, xa_k_norm, xa_wo,
           norm_ffn, ffn_w_gu, ffn_w_down):
    batch, seq, d = x.shape
    mem_len = mem.shape[1]
    depth = w_in.shape[0]
    assert d == D_MODEL and seq % 512 == 0 and (batch * seq) % 1024 == 0 and mem_len % 8 == 0
    row = lambda v: v.reshape(1, -1)

    x2 = x.reshape(batch * seq, d)
    mem2 = mem.reshape(batch * mem_len, d)
    cos_t, sin_t = _rope_tables(positions)
    swa_bias = _swa_bias_table(rel_bias)

    w_in_b, w_kr_b = _in_proj_weight(w_in, norm_mix)
    w_uq_b = _mla_uq_weight(mla_w_uq)
    w_ukv_b = _split_heads_weight(mla_w_ukv, MLA_HEADS, MLA_NOPE)
    w_branch_b, w_out_b = w_branch.astype(BF16), w_out.astype(BF16)
    xa_wkv_b = _split_heads_weight(xa_wkv, XA_HEADS, HEAD_DIM)
    xa_wq_b, xa_wo_b = xa_wq.astype(BF16), xa_wo.astype(BF16)
    w_gu_b = (ffn_w_gu * norm_ffn[:, :, None]).astype(BF16)
    w_down_b = ffn_w_down.astype(BF16)

    for l in range(depth):
        y, kr = _in_proj(x2, w_in_b, w_kr_b, l)
        o_a = _swa_attention(y, swa_bias, row(swa_q_norm[l] * (HEAD_DIM ** -0.5 * LOG2_E)),
                             row(swa_k_norm[l]), row(swa_sinks[l]), batch, seq)
        o_b = _sb_attention(y, batch, seq)
        qm, km, vm = _mla_prep(y, kr, cos_t, sin_t, w_uq_b, w_ukv_b, l,
                               mla_cq_norm[l], mla_ckv_norm[l], mla_q_norm[l], mla_k_norm[l])
        o_c = _mla_attention(qm, km, vm, batch, seq)
        x2 = _merge(o_a, o_b, o_c, y, x2, w_branch_b, w_out_b, l)

        k_mem, v_mem = _mem_kv(mem2, row(norm_mem[l]), xa_wkv_b, l, row(xa_k_norm[l]), batch, mem_len)
        x2 = _xattn(x2, row(norm_xa[l]), xa_wq_b, row(xa_q_norm[l]), k_mem, v_mem, xa_wo_b, l,
                    seq, mem_len)
        x2 = _ffn(x2, w_gu_b, w_down_b, l)
    return x2.reshape(batch, seq, d)
```

```python
import functools
import math

import jax
import jax.numpy as jnp
from jax import lax
from jax.experimental import pallas as pl
from jax.experimental.pallas import tpu as pltpu

F32 = jnp.float32
BF16 = jnp.bfloat16

D_MODEL = 2048
HEAD_DIM = 128
BLOCK = 128
SWA_Q_HEADS = 8
SWA_KV_HEADS = 2
WINDOW = 128
SB_HEADS = 4
MLA_HEADS = 4
MLA_Q_RANK = 512
MLA_KV_RANK = 512
MLA_NOPE = 128
MLA_ROPE = 64
MLA_V = 128
ROPE_THETA = 10000.0
XA_HEADS = 4
REL_BUCKETS = 32
REL_MAX_DIST = 128
D_FF = ((8 * D_MODEL + 3 * 256 - 1) // (3 * 256)) * 256
EPS = 1e-6

WA = SWA_Q_HEADS * HEAD_DIM
KVA = SWA_KV_HEADS * HEAD_DIM
WB = SB_HEADS * HEAD_DIM
WC = MLA_HEADS * MLA_V
XA_W = XA_HEADS * HEAD_DIM
MLA_QK = MLA_NOPE + MLA_ROPE

COL_QA = 0
COL_KA = COL_QA + WA
COL_VA = COL_KA + KVA
COL_QB = COL_VA + KVA
COL_KB = COL_QB + WB
COL_VB = COL_KB + WB
COL_CQ = COL_VB + WB
COL_CKV = COL_CQ + MLA_Q_RANK
COL_GATES = COL_CKV + MLA_KV_RANK
IN_W = COL_GATES + 3 * D_MODEL
IN_TILE_N = 2048

SWA_BLOCKS_PER_STEP = 4
LOG2_E = math.log2(math.e)
SB_SCORE_SCALE = HEAD_DIM ** -0.5 * LOG2_E
SB_SKEW = 1
XA_SUBTILES = 2
ATT_TILE = 256
MASK_VALUE = -1e30
V7X_VMEM_LIMIT_BYTES = 56 * 1024 * 1024


def _params(semantics, flags=None):
    return pltpu.CompilerParams(dimension_semantics=semantics,
                                vmem_limit_bytes=V7X_VMEM_LIMIT_BYTES, flags=flags)


def _rms(x, gain, width):
    ms = jnp.sum(x * x, axis=-1, keepdims=True) * (1.0 / width)
    return x * lax.rsqrt(ms + EPS) * gain


def _dot(a, b):
    return jnp.dot(a, b, preferred_element_type=F32)


def _dot_t(a, b):
    return lax.dot_general(a, b, (((1,), (1,)), ((), ())), preferred_element_type=F32)


def _with_ones(v):
    return jnp.concatenate([v, jnp.ones(v.shape, v.dtype)], axis=1)


def _resident(shape):
    nd = len(shape)
    return pl.BlockSpec(shape, lambda *_: (0,) * nd, pipeline_mode=pl.Buffered(1))


def _resident_layer(stacked_shape, layer):
    nd = len(stacked_shape) - 1
    return pl.BlockSpec((None,) + tuple(stacked_shape[1:]), lambda *_: (layer,) + (0,) * nd,
                        pipeline_mode=pl.Buffered(1))


def _rope_table_kernel(pos_ref, inv_ref, cos_ref, sin_ref):
    ang = pos_ref[...].astype(F32) * inv_ref[...]
    lane = lax.broadcasted_iota(jnp.int32, ang.shape, 1)
    cos_ref[...] = jnp.cos(ang)
    s = jnp.sin(ang)
    sin_ref[...] = jnp.where(lane < HEAD_DIM // 2, -s, s)


def _rope_tables(positions):
    t = positions.size
    half = MLA_ROPE // 2
    inv = ROPE_THETA ** (-jnp.arange(half, dtype=F32) / half)
    inv_l = jnp.tile(inv, HEAD_DIM // half).reshape(1, HEAD_DIM)
    pos_l = jnp.broadcast_to(positions.reshape(t, 1), (t, HEAD_DIM))
    tm = 1024
    return pl.pallas_call(
        _rope_table_kernel,
        grid=(t // tm,),
        in_specs=[pl.BlockSpec((tm, HEAD_DIM), lambda i: (i, 0)),
                  pl.BlockSpec((1, HEAD_DIM), lambda i: (0, 0))],
        out_specs=[pl.BlockSpec((tm, HEAD_DIM), lambda i: (i, 0)),
                   pl.BlockSpec((tm, HEAD_DIM), lambda i: (i, 0))],
        out_shape=[jax.ShapeDtypeStruct((t, HEAD_DIM), F32)] * 2,
        compiler_params=_params(("parallel",)),
        name="rope_tables",
    )(pos_l, inv_l)


def _t5_bucket(rel):
    n = jnp.maximum(rel, 0)
    exact = REL_BUCKETS // 2
    nf = jnp.maximum(n, exact).astype(F32)
    large = exact + (jnp.log(nf / exact) / math.log(REL_MAX_DIST / exact)
                     * (REL_BUCKETS - exact)).astype(jnp.int32)
    large = jnp.minimum(large, REL_BUCKETS - 1)
    return jnp.where(n < exact, n, large)


def _swa_bias_kernel(bucket_ref, rel_bias_ref, out_ref):
    h = pl.program_id(0)
    bucket = bucket_ref[...]
    bias = jnp.zeros(bucket.shape, F32)
    for b in range(REL_BUCKETS):
        bias = jnp.where(bucket == b, rel_bias_ref[b, h] * LOG2_E, bias)
    i = lax.broadcasted_iota(jnp.int32, bucket.shape, 0)
    j = lax.broadcasted_iota(jnp.int32, bucket.shape, 1)
    rel = BLOCK + i - j
    in_window = (rel >= 0) & (rel < WINDOW)
    out_ref[0, 0] = jnp.where(in_window & (j >= BLOCK), bias, MASK_VALUE)
    out_ref[1, 0] = jnp.where(in_window, bias, MASK_VALUE)


def _swa_bias_table(rel_bias):
    i = jnp.arange(BLOCK)[:, None]
    j = jnp.arange(2 * BLOCK)[None, :]
    bucket = _t5_bucket(BLOCK + i - j).astype(jnp.int32)
    return pl.pallas_call(
        _swa_bias_kernel,
        grid=(SWA_Q_HEADS,),
        in_specs=[pl.BlockSpec((BLOCK, 2 * BLOCK), lambda h: (0, 0)),
                  pl.BlockSpec(memory_space=pltpu.SMEM)],
        out_specs=pl.BlockSpec((2, 1, BLOCK, 2 * BLOCK), lambda h: (0, h, 0, 0)),
        out_shape=jax.ShapeDtypeStruct((2, SWA_Q_HEADS, BLOCK, 2 * BLOCK), F32),
        compiler_params=_params(("parallel",)),
        name="swa_bias_table",
    )(bucket, rel_bias)


def _inv_rms(x):
    return lax.rsqrt(jnp.mean(x * x, axis=-1, keepdims=True) + EPS)


def _in_proj_kernel(x_ref, w_ref, wkr_ref, y_ref, kr_ref, xb_ref, r_ref):
    @pl.when(pl.program_id(1) == 0)
    def _():
        x = x_ref[...]
        r = _inv_rms(x)
        xb = x.astype(BF16)
        xb_ref[...] = xb
        r_ref[...] = r
        kr_ref[...] = (_dot(xb, wkr_ref[...]) * r).astype(BF16)

    y_ref[...] = (_dot(xb_ref[...], w_ref[...]) * r_ref[...]).astype(BF16)


def _in_proj(x2, w, w_kr, layer):
    t = x2.shape[0]
    tm, tn = 1024, IN_TILE_N
    return pl.pallas_call(
        _in_proj_kernel,
        grid=(t // tm, IN_W // tn),
        in_specs=[pl.BlockSpec((tm, D_MODEL), lambda i, j: (i, 0)),
                  pl.BlockSpec((None, D_MODEL, tn), lambda i, j: (layer, 0, j)),
                  _resident_layer(w_kr.shape, layer)],
        out_specs=[pl.BlockSpec((tm, tn), lambda i, j: (i, j)),
                   pl.BlockSpec((tm, HEAD_DIM), lambda i, j: (i, 0))],
        out_shape=[jax.ShapeDtypeStruct((t, IN_W), BF16),
                   jax.ShapeDtypeStruct((t, HEAD_DIM), BF16)],
        scratch_shapes=[pltpu.VMEM((tm, D_MODEL), BF16), pltpu.VMEM((tm, 1), F32)],
        compiler_params=_params(("parallel", "arbitrary")),
        name="in_proj",
    )(x2, w, w_kr)


def _swa_kernel(q_ref, kp_ref, kc_ref, vp_ref, vc_ref, bias_ref, qg_ref, kg_ref, sink_ref, o_ref):
    group = SWA_Q_HEADS // SWA_KV_HEADS
    first_variant = jnp.minimum(pl.program_id(1), 1)
    for hk in range(SWA_KV_HEADS):
        cols = slice(hk * HEAD_DIM, (hk + 1) * HEAD_DIM)
        k_all = jnp.concatenate([kp_ref[:, cols], kc_ref[:, cols]], axis=0).astype(F32)
        k_all = _rms(k_all, kg_ref[...], HEAD_DIM).astype(BF16)
        v_all = _with_ones(jnp.concatenate([vp_ref[:, cols], vc_ref[:, cols]], axis=0))
        for blk in range(SWA_BLOCKS_PER_STEP):
            rows = slice(blk * BLOCK, (blk + 1) * BLOCK)
            band = slice(blk * BLOCK, (blk + 2) * BLOCK)
            k, v = k_all[band], v_all[band]
            for g in range(group):
                h = hk * group + g
                hcols = slice(h * HEAD_DIM, (h + 1) * HEAD_DIM)
                q = _rms(q_ref[rows, hcols].astype(F32), qg_ref[...], HEAD_DIM).astype(BF16)
                bias = bias_ref[first_variant, h] if blk == 0 else bias_ref[1, h]
                s = _dot_t(q, k) + bias
                sink = sink_ref[0, h] * LOG2_E
                m = jnp.maximum(jnp.max(s, axis=-1, keepdims=True), sink)
                pv = _dot(jnp.exp2(s - m).astype(BF16), v)
                o = pv[:, :HEAD_DIM] / (pv[:, HEAD_DIM:] + jnp.exp2(sink - m))
                o_ref[rows, hcols] = o.astype(BF16)


def _swa_attention(y, bias, q_gain, k_gain, sinks, batch, seq):
    span = SWA_BLOCKS_PER_STEP * BLOCK
    steps = seq // span
    kcol = COL_KA // KVA
    vcol = COL_VA // KVA
    row = lambda b, n: b * steps + n
    prev = lambda b, n: (b * steps + n) * SWA_BLOCKS_PER_STEP - jnp.minimum(n, 1)
    return pl.pallas_call(
        _swa_kernel,
        grid=(batch, steps),
        in_specs=[pl.BlockSpec((span, WA), lambda b, n: (row(b, n), COL_QA // WA)),
                  pl.BlockSpec((BLOCK, KVA), lambda b, n: (prev(b, n), kcol)),
                  pl.BlockSpec((span, KVA), lambda b, n: (row(b, n), kcol)),
                  pl.BlockSpec((BLOCK, KVA), lambda b, n: (prev(b, n), vcol)),
                  pl.BlockSpec((span, KVA), lambda b, n: (row(b, n), vcol)),
                  _resident(bias.shape),
                  pl.BlockSpec((1, HEAD_DIM), lambda b, n: (0, 0)),
                  pl.BlockSpec((1, HEAD_DIM), lambda b, n: (0, 0)),
                  pl.BlockSpec(memory_space=pltpu.SMEM)],
        out_specs=pl.BlockSpec((span, WA), lambda b, n: (row(b, n), 0)),
        out_shape=jax.ShapeDtypeStruct((batch * seq, WA), BF16),
        compiler_params=_params(("parallel", "arbitrary")),
        name="swa_attention",
    )(y, y, y, y, y, bias, q_gain, k_gain, sinks)


def _sb_steps(q_ref, k_ref, v_ref, o_ref, seq):
    r = lax.broadcasted_iota(jnp.int32, (ATT_TILE, ATT_TILE), 0)
    c = lax.broadcasted_iota(jnp.int32, (ATT_TILE, ATT_TILE), 1)
    later = jnp.where(r > c, 1.0, 0.0).astype(BF16)
    earlier = c < r
    sign_bit = jnp.uint32(0x80000000)
    blocks = [(i, j) for i in range(seq // ATT_TILE) for j in range(i, -1, -1)]
    z2s, own, betweens, runs, accs = {}, {}, {}, {}, {}

    def scores(n):
        i, j = blocks[n]
        q = q_ref[i * ATT_TILE:(i + 1) * ATT_TILE, :]
        z2s[n] = _dot_t(q, k_ref[j * ATT_TILE:(j + 1) * ATT_TILE, :])

    def drops(n):
        i, j = blocks[n]
        z2 = z2s.pop(n)
        neg_abs = pltpu.bitcast(pltpu.bitcast(z2, jnp.uint32) | sign_bit, F32)
        drop2 = jnp.maximum(z2, 0.0) + jnp.log2(1.0 + jnp.exp2(neg_abs))
        if j == i:
            drop2 = jnp.where(earlier, drop2, 0.0)
            runs[n] = jnp.zeros((ATT_TILE, 1), F32)
        if j > 0:
            runs[n + 1] = runs[n] + jnp.sum(drop2, axis=-1, keepdims=True)
        betweens[n] = _dot(drop2.astype(BF16), later)
        own[n] = z2 - drop2

    def weigh(n):
        i, j = blocks[n]
        w = jnp.exp2(own.pop(n) - betweens.pop(n) - runs.pop(n))
        if j == i:
            w = jnp.where(earlier, w, 0.0)
        pv = _dot(w.astype(BF16), v_ref[j * ATT_TILE:(j + 1) * ATT_TILE, :])
        accs[i] = pv if j == i else accs[i] + pv
        if j == 0:
            o_ref[i * ATT_TILE:(i + 1) * ATT_TILE, :] = accs.pop(i).astype(BF16)

    def step(t):
        if t < len(blocks):
            scores(t)
        if 0 <= t - SB_SKEW < len(blocks):
            drops(t - SB_SKEW)
        if 0 <= t - 2 * SB_SKEW < len(blocks):
            weigh(t - 2 * SB_SKEW)

    return [functools.partial(step, t) for t in range(len(blocks) + 2 * SB_SKEW)]


def _spread_rope(v):
    half = MLA_ROPE // 2
    zeros = jnp.zeros(v.shape[:-1] + (HEAD_DIM // 2 - half,), v.dtype)
    return jnp.concatenate([v[..., :half], zeros, v[..., half:], zeros], axis=-1)


def _rope(x, cos, sin_signed):
    return x * cos + pltpu.roll(x, HEAD_DIM // 2, 1) * sin_signed


def _mla_prep_kernel(cq_ref, ckv_ref, kr_ref, cos_ref, sin_ref, wuq_ref, wukv_ref,
                     cqg_ref, ckvg_ref, qgn_ref, qgr_ref, kgn_ref, kgr_ref,
                     q_ref, k_ref, v_ref):
    scale = MLA_QK ** -0.5 * math.log2(math.e)
    rope0 = MLA_HEADS * MLA_NOPE
    cos, sin = cos_ref[...], sin_ref[...]
    cqn = _rms(cq_ref[...].astype(F32), cqg_ref[...], MLA_Q_RANK).astype(BF16)
    ckvn = _rms(ckv_ref[...].astype(F32), ckvg_ref[...], MLA_KV_RANK).astype(BF16)
    qc = _dot(cqn, wuq_ref[...])
    kvc = _dot(ckvn, wukv_ref[...])
    k_rope = _rope(_rms(kr_ref[...].astype(F32), kgr_ref[...], MLA_ROPE), cos, sin).astype(BF16)
    for h in range(MLA_HEADS):
        nope = slice(h * MLA_NOPE, (h + 1) * MLA_NOPE)
        spread_rope = slice(rope0 + h * HEAD_DIM, rope0 + (h + 1) * HEAD_DIM)
        qn = _rms(qc[:, nope], qgn_ref[...] * scale, MLA_NOPE)
        qr = _rope(_rms(qc[:, spread_rope], qgr_ref[...] * scale, MLA_ROPE), cos, sin)
        q_ref[:, 2 * h * HEAD_DIM:(2 * h + 1) * HEAD_DIM] = qn.astype(BF16)
        q_ref[:, (2 * h + 1) * HEAD_DIM:(2 * h + 2) * HEAD_DIM] = qr.astype(BF16)
        kn = _rms(kvc[:, nope], kgn_ref[...], MLA_NOPE)
        k_ref[:, 2 * h * HEAD_DIM:(2 * h + 1) * HEAD_DIM] = kn.astype(BF16)
        k_ref[:, (2 * h + 1) * HEAD_DIM:(2 * h + 2) * HEAD_DIM] = k_rope
    v_ref[...] = kvc[:, rope0:].astype(BF16)


def _mla_prep(y, kr, cos_t, sin_t, w_uq, w_ukv, layer, cq_gain, ckv_gain, q_gain, k_gain):
    t = y.shape[0]
    tm = 512
    row = lambda v: v.reshape(1, -1)
    gains = [row(cq_gain), row(ckv_gain),
             row(q_gain[:MLA_NOPE]), row(_spread_rope(q_gain[MLA_NOPE:])),
             row(k_gain[:MLA_NOPE]), row(_spread_rope(k_gain[MLA_NOPE:]))]
    wq = 2 * MLA_HEADS * HEAD_DIM
    return pl.pallas_call(
        _mla_prep_kernel,
        grid=(t // tm,),
        in_specs=[pl.BlockSpec((tm, MLA_Q_RANK), lambda i: (i, COL_CQ // MLA_Q_RANK)),
                  pl.BlockSpec((tm, MLA_KV_RANK), lambda i: (i, COL_CKV // MLA_KV_RANK)),
                  pl.BlockSpec((tm, HEAD_DIM), lambda i: (i, 0)),
                  pl.BlockSpec((tm, HEAD_DIM), lambda i: (i, 0)),
                  pl.BlockSpec((tm, HEAD_DIM), lambda i: (i, 0)),
                  _resident_layer(w_uq.shape, layer), _resident_layer(w_ukv.shape, layer)]
                 + [_resident(g.shape) for g in gains],
        out_specs=[pl.BlockSpec((tm, wq), lambda i: (i, 0)),
                   pl.BlockSpec((tm, wq), lambda i: (i, 0)),
                   pl.BlockSpec((tm, WC), lambda i: (i, 0))],
        out_shape=[jax.ShapeDtypeStruct((t, wq), BF16),
                   jax.ShapeDtypeStruct((t, wq), BF16),
                   jax.ShapeDtypeStruct((t, WC), BF16)],
        compiler_params=_params(("parallel",)),
        name="mla_prep",
    )(y, y, kr, cos_t, sin_t, w_uq, w_ukv, *gains)


def _mla_steps(q_ref, k_ref, v_ref, o_ref, seq):
    r = lax.broadcasted_iota(jnp.int32, (ATT_TILE, ATT_TILE), 0)
    c = lax.broadcasted_iota(jnp.int32, (ATT_TILE, ATT_TILE), 1)
    causal = c <= r
    tiles = seq // ATT_TILE
    scores, row_max, probs = {}, {}, {}

    def score(i, j):
        q = q_ref[i * ATT_TILE:(i + 1) * ATT_TILE, :]
        s = _dot_t(q, k_ref[j * ATT_TILE:(j + 1) * ATT_TILE, :])
        scores[i, j] = jnp.where(causal, s, MASK_VALUE) if j == i else s

    def maximum(i):
        tile_scores = [scores[i, j] for j in range(i + 1)]
        row_max[i] = jnp.max(functools.reduce(jnp.maximum, tile_scores), axis=-1, keepdims=True)

    def prob(i, j):
        probs[i, j] = jnp.exp2(scores.pop((i, j)) - row_max[i]).astype(BF16)

    def attend(i):
        p_all = jnp.concatenate([probs.pop((i, j)) for j in range(i + 1)], axis=1)
        acc = _dot(p_all, _with_ones(v_ref[0:(i + 1) * ATT_TILE, :]))
        o_ref[i * ATT_TILE:(i + 1) * ATT_TILE, :] = (acc[:, :MLA_V] / acc[:, MLA_V:]).astype(BF16)

    order = list(range(tiles - 1, -1, -1))
    steps = []
    for t in range(tiles + 1):
        ahead = [functools.partial(score, order[t], j) for j in range(order[t] + 1)] if t < tiles else []
        behind = []
        if t >= 1:
            i = order[t - 1]
            behind = ([functools.partial(maximum, i)]
                      + [functools.partial(prob, i, j) for j in range(i + 1)]
                      + [functools.partial(attend, i)])
        while ahead or behind:
            if ahead:
                steps.append((ahead.pop(0), 1))
            if behind:
                steps.append((behind.pop(0), 1))
    return steps


def _sb_mla_kernel(qb_ref, kb_ref, vb_ref, qm_ref, km_ref, vm_ref, ob_ref, oc_ref, *, seq):
    sb = _sb_steps(qb_ref, kb_ref, vb_ref, ob_ref, seq)
    mla = _mla_steps(qm_ref, km_ref, vm_ref, oc_ref, seq)
    total = sum(cost for _, cost in mla)
    issued, spent = 0, 0
    for mla_step, cost in mla:
        mla_step()
        spent += cost
        target = len(sb) * spent // total
        while issued < target:
            sb[issued]()
            issued += 1


def _sb_mla_attention(y, qm, km, vm, batch, seq):
    assert SB_HEADS == MLA_HEADS
    qc, kc, vc = COL_QB // HEAD_DIM, COL_KB // HEAD_DIM, COL_VB // HEAD_DIM
    wqk = 2 * HEAD_DIM
    return pl.pallas_call(
        functools.partial(_sb_mla_kernel, seq=seq),
        grid=(batch, SB_HEADS),
        in_specs=[pl.BlockSpec((seq, HEAD_DIM), lambda b, h: (b, qc + h)),
                  pl.BlockSpec((seq, HEAD_DIM), lambda b, h: (b, kc + h)),
                  pl.BlockSpec((seq, HEAD_DIM), lambda b, h: (b, vc + h)),
                  pl.BlockSpec((seq, wqk), lambda b, h: (b, h)),
                  pl.BlockSpec((seq, wqk), lambda b, h: (b, h)),
                  pl.BlockSpec((seq, MLA_V), lambda b, h: (b, h))],
        out_specs=[pl.BlockSpec((seq, HEAD_DIM), lambda b, h: (b, h)),
                   pl.BlockSpec((seq, MLA_V), lambda b, h: (b, h))],
        out_shape=[jax.ShapeDtypeStruct((batch * seq, WB), BF16),
                   jax.ShapeDtypeStruct((batch * seq, WC), BF16)],
        compiler_params=_params(("parallel", "parallel")),
        name="sb_mla_attention",
    )(y, y, y, qm, km, vm)


def _merge_kernel(oa_ref, ob_ref, oc_ref, ga_ref, gb_ref, gc_ref, x_ref, wbr_ref, wout_ref, o_ref):
    ma = _dot(oa_ref[...], wbr_ref[0:WA, :])
    mb = _dot(ob_ref[...], wbr_ref[WA:WA + WB, :])
    mc = _dot(oc_ref[...], wbr_ref[WA + WB:, :])
    merged = (jax.nn.sigmoid(ga_ref[...].astype(F32)) * ma
              + jax.nn.sigmoid(gb_ref[...].astype(F32)) * mb
              + jax.nn.sigmoid(gc_ref[...].astype(F32)) * mc)
    o_ref[...] = x_ref[...] + _dot(merged.astype(BF16), wout_ref[...])


def _merge(o_a, o_b, o_c, y, x2, w_branch, w_out, layer):
    t = x2.shape[0]
    tm = 512
    gcol = COL_GATES // D_MODEL
    return pl.pallas_call(
        _merge_kernel,
        grid=(t // tm,),
        in_specs=[pl.BlockSpec((tm, WA), lambda i: (i, 0)),
                  pl.BlockSpec((tm, WB), lambda i: (i, 0)),
                  pl.BlockSpec((tm, WC), lambda i: (i, 0)),
                  pl.BlockSpec((tm, D_MODEL), lambda i: (i, gcol)),
                  pl.BlockSpec((tm, D_MODEL), lambda i: (i, gcol + 1)),
                  pl.BlockSpec((tm, D_MODEL), lambda i: (i, gcol + 2)),
                  pl.BlockSpec((tm, D_MODEL), lambda i: (i, 0)),
                  _resident_layer(w_branch.shape, layer), _resident_layer(w_out.shape, layer)],
        out_specs=pl.BlockSpec((tm, D_MODEL), lambda i: (i, 0)),
        out_shape=jax.ShapeDtypeStruct((t, D_MODEL), F32),
        compiler_params=_params(("parallel",)),
        name="gated_merge",
    )(o_a, o_b, o_c, y, y, y, x2, w_branch, w_out)


def _mem_kv_kernel(mem_ref, g_ref, wkv_ref, kg_ref, k_ref, v_ref):
    memn = _rms(mem_ref[...], g_ref[...], D_MODEL).astype(BF16)
    kv = _dot(memn, wkv_ref[...])
    for h in range(XA_HEADS):
        cols = slice(h * HEAD_DIM, (h + 1) * HEAD_DIM)
        k_ref[:, cols] = _rms(kv[:, cols], kg_ref[...], HEAD_DIM).astype(BF16)
    v_ref[...] = kv[:, XA_W:].astype(BF16)


def _mem_kv(mem2, gain, w_kv, layer, k_gain, batch, mem_len):
    return pl.pallas_call(
        _mem_kv_kernel,
        grid=(batch,),
        in_specs=[pl.BlockSpec((mem_len, D_MODEL), lambda b: (b, 0)),
                  _resident(gain.shape), _resident_layer(w_kv.shape, layer), _resident(k_gain.shape)],
        out_specs=[pl.BlockSpec((mem_len, XA_W), lambda b: (b, 0)),
                   pl.BlockSpec((mem_len, XA_W), lambda b: (b, 0))],
        out_shape=[jax.ShapeDtypeStruct((batch * mem_len, XA_W), BF16)] * 2,
        compiler_params=_params(("parallel",)),
        name="mem_kv",
    )(mem2, gain, w_kv, k_gain)


def _xattn_kernel(x_ref, g_ref, wq_ref, qg_ref, k_ref, v_ref, wo_ref, o_ref):
    qg = qg_ref[...] * (HEAD_DIM ** -0.5 * LOG2_E)
    sub = x_ref.shape[0] // XA_SUBTILES
    stages = [{} for _ in range(4)]

    def project(n):
        rows = slice(n * sub, (n + 1) * sub)
        xn = _rms(x_ref[rows, :], g_ref[...], D_MODEL).astype(BF16)
        stages[0][n] = _dot(xn, wq_ref[...])

    def score(n):
        q = stages[0].pop(n)
        cols = [slice(h * HEAD_DIM, (h + 1) * HEAD_DIM) for h in range(XA_HEADS)]
        stages[1][n] = [_dot_t(_rms(q[:, c], qg, HEAD_DIM).astype(BF16), k_ref[:, c]) for c in cols]

    def attend(n):
        heads = []
        for h, s in enumerate(stages[1].pop(n)):
            p = jnp.exp2(s - jnp.max(s, axis=-1, keepdims=True))
            pv = _dot(p.astype(BF16), v_ref[:, h * HEAD_DIM:(h + 1) * HEAD_DIM])
            heads.append((pv / jnp.sum(p, axis=-1, keepdims=True)).astype(BF16))
        stages[2][n] = jnp.concatenate(heads, axis=-1)

    def output(n):
        rows = slice(n * sub, (n + 1) * sub)
        o_ref[rows, :] = x_ref[rows, :] + _dot(stages[2].pop(n), wo_ref[...])

    steps = [project, score, attend, output]
    for t in range(XA_SUBTILES + len(steps) - 1):
        for depth, step in enumerate(steps):
            if 0 <= t - depth < XA_SUBTILES:
                step(t - depth)


def _xattn(x2, gain, w_q, q_gain, k_mem, v_mem, w_o, layer, seq, mem_len):
    t = x2.shape[0]
    tm = 512
    per_batch = seq // tm
    return pl.pallas_call(
        _xattn_kernel,
        grid=(t // tm,),
        in_specs=[pl.BlockSpec((tm, D_MODEL), lambda i: (i, 0)),
                  _resident(gain.shape), _resident_layer(w_q.shape, layer), _resident(q_gain.shape),
                  pl.BlockSpec((mem_len, XA_W), lambda i: (i // per_batch, 0)),
                  pl.BlockSpec((mem_len, XA_W), lambda i: (i // per_batch, 0)),
                  _resident_layer(w_o.shape, layer)],
        out_specs=pl.BlockSpec((tm, D_MODEL), lambda i: (i, 0)),
        out_shape=jax.ShapeDtypeStruct((t, D_MODEL), F32),
        compiler_params=_params(("parallel",)),
        name="mem_xattn",
    )(x2, gain, w_q, q_gain, k_mem, v_mem, w_o)


def _ffn_kernel(x_ref, wg_ref, wu_ref, wd_ref, o_ref, xb_ref, r_ref):
    @pl.when(pl.program_id(1) == 0)
    def _():
        x = x_ref[...]
        xb_ref[...] = x.astype(BF16)
        r_ref[...] = _inv_rms(x)
        o_ref[...] = x

    xb, r = xb_ref[...], r_ref[...]
    gate = _dot(xb, wg_ref[...]) * r
    up = _dot(xb, wu_ref[...]) * r
    act = (gate * jax.nn.sigmoid(gate) * up).astype(BF16)
    o_ref[...] += _dot(act, wd_ref[...])


def _ffn(x2, w_gu, w_down, layer):
    t = x2.shape[0]
    tm, tf = 1024, 512
    nf = D_FF // tf
    return pl.pallas_call(
        _ffn_kernel,
        grid=(t // tm, nf),
        in_specs=[pl.BlockSpec((tm, D_MODEL), lambda i, f: (i, 0)),
                  pl.BlockSpec((None, D_MODEL, tf), lambda i, f: (layer, 0, f)),
                  pl.BlockSpec((None, D_MODEL, tf), lambda i, f: (layer, 0, nf + f)),
                  pl.BlockSpec((None, tf, D_MODEL), lambda i, f: (layer, f, 0))],
        out_specs=pl.BlockSpec((tm, D_MODEL), lambda i, f: (i, 0)),
        out_shape=jax.ShapeDtypeStruct((t, D_MODEL), F32),
        scratch_shapes=[pltpu.VMEM((tm, D_MODEL), BF16), pltpu.VMEM((tm, 1), F32)],
        compiler_params=_params(("parallel", "arbitrary")),
        name="swiglu_ffn",
    )(x2, w_gu, w_gu, w_down)


W_IN_COL_BLOCK = 512


def _in_proj_weight_kernel(*refs):
    *wt_refs, g_ref, o_ref = refs
    j = pl.program_id(1)
    sb_q = (j >= COL_QB // W_IN_COL_BLOCK) & (j < (COL_QB + WB) // W_IN_COL_BLOCK)
    scale = g_ref[...] * jnp.where(sb_q, SB_SCORE_SCALE, 1.0)
    wt = jnp.concatenate([r[...] for r in wt_refs], axis=0)
    o_ref[...] = (wt * scale).T.astype(BF16)


def _in_proj_weight(w, gain):
    depth, d, _ = w.shape
    pieces = W_IN_COL_BLOCK // MLA_ROPE
    main_blocks = COL_GATES // W_IN_COL_BLOCK
    piece_spec = lambda k: pl.BlockSpec(
        (None, MLA_ROPE, d), lambda l, j: (l, pieces * j + k + jnp.where(j < main_blocks, 0, 1), 0))
    wt = jnp.swapaxes(w, 1, 2)
    w_main = pl.pallas_call(
        _in_proj_weight_kernel,
        grid=(depth, IN_W // W_IN_COL_BLOCK),
        in_specs=[piece_spec(k) for k in range(pieces)]
                 + [pl.BlockSpec((None, 1, d), lambda l, j: (l, 0, 0))],
        out_specs=pl.BlockSpec((None, d, W_IN_COL_BLOCK), lambda l, j: (l, 0, j)),
        out_shape=jax.ShapeDtypeStruct((depth, d, IN_W), BF16),
        compiler_params=_params(("parallel", "parallel")),
        name="in_proj_weight_layout",
    )(*([wt] * pieces), gain.reshape(depth, 1, d))
    w_kr = w[:, :, COL_GATES:COL_GATES + MLA_ROPE] * gain[:, :, None]
    return w_main, _spread_rope(w_kr).astype(BF16)


def _mla_uq_weight(w):
    lead = w.shape[:-1]
    w = w.reshape(lead + (MLA_HEADS, MLA_QK))
    nope = w[..., :MLA_NOPE].reshape(lead + (MLA_HEADS * MLA_NOPE,))
    rope = _spread_rope(w[..., MLA_NOPE:]).reshape(lead + (MLA_HEADS * HEAD_DIM,))
    return jnp.concatenate([nope, rope], axis=-1).astype(BF16)


def _split_heads_weight(w, heads, first):
    lead = w.shape[:-1]
    w = w.reshape(lead + (heads, -1))
    a = w[..., :first].reshape(lead + (-1,))
    b = w[..., first:].reshape(lead + (-1,))
    return jnp.concatenate([a, b], axis=-1).astype(BF16)


def kernel(x, mem, positions, rel_bias, norm_mix, w_in, swa_q_norm, swa_k_norm, swa_sinks,
           mla_cq_norm, mla_ckv_norm, mla_w_uq, mla_w_ukv, mla_q_norm, mla_k_norm,
           w_branch, w_out, norm_xa, norm_mem, xa_wq, xa_wkv, xa_q_norm, xa_k_norm, xa_wo,
           norm_ffn, ffn_w_gu, ffn_w_down):
    batch, seq, d = x.shape
    mem_len = mem.shape[1]
    depth = w_in.shape[0]
    assert d == D_MODEL and seq % 512 == 0 and (batch * seq) % 1024 == 0 and mem_len % 8 == 0
    row = lambda v: v.reshape(1, -1)

    x2 = x.reshape(batch * seq, d)
    mem2 = mem.reshape(batch * mem_len, d)
    cos_t, sin_t = _rope_tables(positions)
    swa_bias = _swa_bias_table(rel_bias)

    w_in_b, w_kr_b = _in_proj_weight(w_in, norm_mix)
    w_uq_b = _mla_uq_weight(mla_w_uq)
    w_ukv_b = _split_heads_weight(mla_w_ukv, MLA_HEADS, MLA_NOPE)
    w_branch_b, w_out_b = w_branch.astype(BF16), w_out.astype(BF16)
    xa_wkv_b = _split_heads_weight(xa_wkv, XA_HEADS, HEAD_DIM)
    xa_wq_b, xa_wo_b = xa_wq.astype(BF16), xa_wo.astype(BF16)
    w_gu_b = (ffn_w_gu * norm_ffn[:, :, None]).astype(BF16)
    w_down_b = ffn_w_down.astype(BF16)

    for l in range(depth):
        y, kr = _in_proj(x2, w_in_b, w_kr_b, l)
        o_a = _swa_attention(y, swa_bias, row(swa_q_norm[l] * (HEAD_DIM ** -0.5 * LOG2_E)),
                             row(swa_k_norm[l]), row(swa_sinks[l]), batch, seq)
        qm, km, vm = _mla_prep(y, kr, cos_t, sin_t, w_uq_b, w_ukv_b, l,
                               mla_cq_norm[l], mla_ckv_norm[l], mla_q_norm[l], mla_k_norm[l])
        o_b, o_c = _sb_mla_attention(y, qm, km, vm, batch, seq)
        x2 = _merge(o_a, o_b, o_c, y, x2, w_branch_b, w_out_b, l)

        k_mem, v_mem = _mem_kv(mem2, row(norm_mem[l]), xa_wkv_b, l, row(xa_k_norm[l]), batch, mem_len)
        x2 = _xattn(x2, row(norm_xa[l]), xa_wq_b, row(xa_q_norm[l]), k_mem, v_mem, xa_wo_b, l,
                    seq, mem_len)
        x2 = _ffn(x2, w_gu_b, w_down_b, l)
    return x2.reshape(batch, seq, d)
```
